```python
import jax
import jax.numpy as jnp
from jax import lax
import numpy as np


D_MODEL = 2048
BATCH = 1
SEQ = 8192
DEPTH = 1
DEC_BATCH = 32
DEC_SEQ = 16
PAST_LEN = 1024

CHUNK = 64
Q_BLOCK = 128
N_HEADS = 8
QK_NOPE = 128
QK_ROPE = 64
V_HEAD = 128
Q_LORA = 512
KV_LORA = 512
ROPE_THETA = 10000.0
ATTN_SCALE = (QK_NOPE + QK_ROPE) ** -0.5
CONV_WIDTH = 1024
CONV_K = 3
PEER_HEADS = 8
PEER_QDIM = 256
PEER_HALF = PEER_QDIM // 2
N_KEYS = 128
N_EXPERTS = N_KEYS * N_KEYS
PEER_TOPK = 16
PEER_BLOCK = 128
N_BRANCH = 2
EPS = 1e-6
IN_COLS = Q_LORA + KV_LORA + QK_ROPE + 3 * CONV_WIDTH + N_BRANCH * D_MODEL
IN_SPLITS = [Q_LORA, Q_LORA + KV_LORA, Q_LORA + KV_LORA + QK_ROPE,
             Q_LORA + KV_LORA + QK_ROPE + CONV_WIDTH,
             Q_LORA + KV_LORA + QK_ROPE + 2 * CONV_WIDTH,
             Q_LORA + KV_LORA + QK_ROPE + 3 * CONV_WIDTH]

kernel_name = 'streaming_mla_shortconv_peer'


def rms_norm(x, g):
    xf = x.astype(jnp.float32)
    y = xf * lax.rsqrt(jnp.mean(xf * xf, axis=-1, keepdims=True) + EPS)
    return y.astype(x.dtype) * g


def modulate(h, shift, scale):
    return h * (1 + scale[:, None, :]) + shift[:, None, :]


def rope(x, pos):
    half = x.shape[-1] // 2
    inv = 1.0 / (ROPE_THETA ** (jnp.arange(half, dtype=jnp.float32) / half))
    ang = pos.astype(jnp.float32)[:, None] * inv[None, :]
    ang = ang.reshape((ang.shape[0],) + (1,) * (x.ndim - 3) + (half,))
    cos = jnp.cos(ang).astype(x.dtype)
    sin = jnp.sin(ang).astype(x.dtype)
    x1, x2 = x[..., :half], x[..., half:]
    return jnp.concatenate([x1 * cos - x2 * sin, x1 * sin + x2 * cos], axis=-1)


def mla_attend(q_nope, q_rope, k_nope, k_rope, v, q_pos, k_pos):
    s = (jnp.einsum('bqhd,bkhd->bhqk', q_nope, k_nope)
         + jnp.einsum('bqhd,bkd->bhqk', q_rope, k_rope))
    s = s.astype(jnp.float32) * ATTN_SCALE
    mask = (k_pos // CHUNK)[None, :] <= (q_pos // CHUNK)[:, None]
    s = jnp.where(mask[None, None], s, jnp.finfo(jnp.float32).min)
    p = jax.nn.softmax(s, axis=-1).astype(v.dtype)
    return jnp.einsum('bhqk,bkhd->bqhd', p, v)


def peer_tokens(t, w_pq, sub_k1, sub_k2, w_u, w_v):
    n = t.shape[0]
    q = (t @ w_pq).reshape(n, PEER_HEADS, PEER_QDIM)
    s1 = jnp.einsum('thd,hnd->thn', q[..., :PEER_HALF], sub_k1)
    s2 = jnp.einsum('thd,hnd->thn', q[..., PEER_HALF:], sub_k2)
    v1, i1 = lax.top_k(s1, PEER_TOPK)
    v2, i2 = lax.top_k(s2, PEER_TOPK)
    cand = (v1[..., :, None] + v2[..., None, :]).reshape(n, PEER_HEADS, PEER_TOPK * PEER_TOPK)
    cidx = (i1[..., :, None] * N_KEYS + i2[..., None, :]).reshape(n, PEER_HEADS, PEER_TOPK * PEER_TOPK)
    sv, sel = lax.top_k(cand, PEER_TOPK)
    eidx = jnp.take_along_axis(cidx, sel, axis=-1)
    gate = jax.nn.softmax(sv.astype(jnp.float32), axis=-1).astype(t.dtype)
    act = jax.nn.gelu(jnp.einsum('thkd,td->thk', w_u[eidx], t), approximate=False)
    return jnp.einsum('thk,thkd->td', gate * act, w_v[eidx])


def peer(h, w_pq, sub_k1, sub_k2, w_u, w_v):
    b, s, d = h.shape
    n = b * s
    nb = -(-n // PEER_BLOCK)
    flat = jnp.pad(h.reshape(n, d), ((0, nb * PEER_BLOCK - n), (0, 0)))
    out = lax.map(lambda t: peer_tokens(t, w_pq, sub_k1, sub_k2, w_u, w_v),
                  flat.reshape(nb, PEER_BLOCK, d))
    return out.reshape(nb * PEER_BLOCK, d)[:n].reshape(b, s, d)


def trunk_layer(x, c, pos, past_ckv, past_krope, conv_left,
                w_ada, b_ada, g_n1, w_in, g_q, g_kv, w_uq, w_uk, w_uv, w_oa,
                w_conv, b_conv, w_ob, w_o, g_n2, w_pq, sub_k1, sub_k2, w_u, w_v):
    b, s, d = x.shape
    mod = c @ w_ada + b_ada
    sh1, sc1, gt1, sh2, sc2, gt2 = jnp.split(mod, 6, axis=-1)
    h = modulate(rms_norm(x, g_n1), sh1, sc1)
    p = h @ w_in
    p_q, p_kv, p_kr, p_h, p_b, p_c, p_g = jnp.split(p, IN_SPLITS, axis=-1)

    c_q = rms_norm(p_q, g_q)
    q = jnp.einsum('bsr,rhd->bshd', c_q, w_uq)
    q_nope = q[..., :QK_NOPE]
    q_rope = rope(q[..., QK_NOPE:], pos)
    ckv = rms_norm(p_kv, g_kv)
    krope = rope(p_kr, pos)
    if past_ckv is None:
        keys_ckv, keys_kr, k_pos = ckv, krope, pos
    else:
        keys_ckv = jnp.concatenate([past_ckv, ckv], axis=1)
        keys_kr = jnp.concatenate([past_krope, krope], axis=1)
        k_pos = jnp.arange(keys_ckv.shape[1])
    k_nope = jnp.einsum('bkr,rhd->bkhd', keys_ckv, w_uk)
    v = jnp.einsum('bkr,rhd->bkhd', keys_ckv, w_uv)
    if s > Q_BLOCK and s % Q_BLOCK == 0:
        nb = s // Q_BLOCK
        qn = q_nope.reshape(b, nb, Q_BLOCK, N_HEADS, QK_NOPE).swapaxes(0, 1)
        qr = q_rope.reshape(b, nb, Q_BLOCK, N_HEADS, QK_ROPE).swapaxes(0, 1)
        qp = pos.reshape(nb, Q_BLOCK)
        o = lax.map(lambda a: mla_attend(a[0], a[1], k_nope, keys_kr, v, a[2], k_pos), (qn, qr, qp))
        o = o.swapaxes(0, 1).reshape(b, s, N_HEADS * V_HEAD)
    else:
        o = mla_attend(q_nope, q_rope, k_nope, keys_kr, v, pos, k_pos).reshape(b, s, N_HEADS * V_HEAD)
    branch_a = o @ w_oa

    z = p_c * p_h
    zpad = jnp.concatenate([conv_left, z], axis=1)
    yc = (w_conv[0] * zpad[:, 0:s] + w_conv[1] * zpad[:, 1:s + 1]
          + w_conv[2] * zpad[:, 2:s + 2] + b_conv)
    branch_b = (p_b * yc) @ w_ob
    new_conv = zpad[:, zpad.shape[1] - (CONV_K - 1):]

    g = jax.nn.sigmoid(p_g).reshape(b, s, N_BRANCH, d)
    merged = g[:, :, 0] * branch_a + g[:, :, 1] * branch_b
    x = x + gt1[:, None, :] * (merged @ w_o)

    h2 = modulate(rms_norm(x, g_n2), sh2, sc2)
    x = x + gt2[:, None, :] * peer(h2, w_pq, sub_k1, sub_k2, w_u, w_v)
    return x, ckv, krope, new_conv


def setup_inputs(seed: int = 0):
    key = jax.random.key(seed)
    ks = jax.random.split(key, 32)
    f32 = jnp.float32

    def nrm(k, shape, scale):
        return jax.random.normal(k, shape, f32) * scale

    def gain(k, shape):
        return 1.0 + 0.01 * jax.random.normal(k, shape, f32)

    L, D, W = DEPTH, D_MODEL, CONV_WIDTH
    return {
        'x_prompt': nrm(ks[0], (BATCH, SEQ, D), 1.0),
        'x_sample': nrm(ks[1], (DEC_BATCH, DEC_SEQ, D), 1.0),
        'cache_ckv': nrm(ks[2], (L, DEC_BATCH, PAST_LEN, KV_LORA), 1.0),
        'cache_krope': nrm(ks[3], (L, DEC_BATCH, PAST_LEN, QK_ROPE), 1.0),
        'state_conv': nrm(ks[4], (L, DEC_BATCH, CONV_K - 1, W), 0.5),
        'c_prompt': nrm(ks[5], (BATCH, D), 1.0),
        'c_sample': nrm(ks[6], (DEC_BATCH, D), 1.0),
        'w_ada': nrm(ks[7], (L, D, 6 * D), 0.2 * D ** -0.5),
        'b_ada': nrm(ks[8], (L, 6 * D), 0.01),
        'g_n1': gain(ks[9], (L, D)),
        'w_in': nrm(ks[10], (L, D, IN_COLS), D ** -0.5),
        'g_q': gain(ks[11], (L, Q_LORA)),
        'g_kv': gain(ks[12], (L, KV_LORA)),
        'w_uq': nrm(ks[13], (L, Q_LORA, N_HEADS, QK_NOPE + QK_ROPE), Q_LORA ** -0.5),
        'w_uk': nrm(ks[14], (L, KV_LORA, N_HEADS, QK_NOPE), KV_LORA ** -0.5),
        'w_uv': nrm(ks[15], (L, KV_LORA, N_HEADS, V_HEAD), KV_LORA ** -0.5),
        'w_oa': nrm(ks[16], (L, N_HEADS * V_HEAD, D), (N_HEADS * V_HEAD) ** -0.5),
        'w_conv': nrm(ks[17], (L, CONV_K, W), CONV_K ** -0.5),
        'b_conv': nrm(ks[18], (L, W), 0.01),
        'w_ob': nrm(ks[19], (L, W, D), W ** -0.5),
        'w_o': nrm(ks[20], (L, D, D), D ** -0.5),
        'g_n2': gain(ks[21], (L, D)),
        'w_pq': nrm(ks[22], (L, D, PEER_HEADS * PEER_QDIM), D ** -0.5),
        'sub_k1': nrm(ks[23], (L, PEER_HEADS, N_KEYS, PEER_HALF), PEER_HALF ** -0.5),
        'sub_k2': nrm(ks[24], (L, PEER_HEADS, N_KEYS, PEER_HALF), PEER_HALF ** -0.5),
        'w_u': nrm(ks[25], (L, N_EXPERTS, D), D ** -0.5),
        'w_v': nrm(ks[26], (L, N_EXPERTS, D), PEER_HEADS ** -0.5),
        'g_f': gain(ks[27], (D,)),
    }


def reference(x_prompt, x_sample, cache_ckv, cache_krope, state_conv, c_prompt, c_sample,
              w_ada, b_ada, g_n1, w_in, g_q, g_kv, w_uq, w_uk, w_uv, w_oa,
              w_conv, b_conv, w_ob, w_o, g_n2, w_pq, sub_k1, sub_k2, w_u, w_v, g_f):
    past = cache_ckv.shape[2]
    pos_p = jnp.arange(x_prompt.shape[1])
    pos_s = past + jnp.arange(x_sample.shape[1])
    xp, xs = x_prompt, x_sample
    left_p = jnp.zeros((xp.shape[0], CONV_K - 1, CONV_WIDTH), xp.dtype)
    ckv_p, kr_p, conv_p, ckv_s, kr_s, conv_s = [], [], [], [], [], []
    for l in range(DEPTH):
        wl = (w_ada[l], b_ada[l], g_n1[l], w_in[l], g_q[l], g_kv[l], w_uq[l], w_uk[l], w_uv[l],
              w_oa[l], w_conv[l], b_conv[l], w_ob[l], w_o[l], g_n2[l], w_pq[l], sub_k1[l],
              sub_k2[l], w_u[l], w_v[l])
        xp, a1, a2, a3 = trunk_layer(xp, c_prompt, pos_p, None, None, left_p, *wl)
        xs, b1, b2, b3 = trunk_layer(xs, c_sample, pos_s, cache_ckv[l], cache_krope[l], state_conv[l], *wl)
        ckv_p.append(a1); kr_p.append(a2); conv_p.append(a3)
        ckv_s.append(b1); kr_s.append(b2); conv_s.append(b3)
    y_prompt = rms_norm(xp, g_f)
    y_sample = rms_norm(xs, g_f)
    new_ckv_p = jnp.stack(ckv_p)
    new_kr_p = jnp.stack(kr_p)
    new_conv_p = jnp.stack(conv_p)
    new_ckv_s = jnp.stack(ckv_s)
    new_kr_s = jnp.stack(kr_s)
    new_conv_s = jnp.stack(conv_s)
    return (y_prompt, y_sample, new_ckv_p, new_kr_p, new_conv_p, new_ckv_s, new_kr_s, new_conv_s)
```

```python
import functools

import jax
import jax.numpy as jnp
from jax import lax
from jax.experimental import pallas as pl
from jax.experimental.pallas import tpu as pltpu

F32 = jnp.float32
BF16 = jnp.bfloat16

D = 2048
TP = 8192
NB = 32
SS = 16
TS = NB * SS
T = TP + TS
PAST = 1024
CHUNK = 64
NH = 8
DN = 128
DR = 64
DQ = DN + DR
DV = 128
QL = 512
KVL = 512
ROPE_THETA = 10000.0
SCALE = (DN + DR) ** -0.5
CW = 1024
PH = 8
NK = 128
NE = NK * NK
TOPK = 16
EPS = 1e-6
NEG = float(jnp.finfo(jnp.float32).min)

TM = 256
NPT = TP // TM
NT_ = T // TM
MODROWS = 40
PROMPT_ROW = NB

TQ = 256
TK = 256
TMP = 512
NI = 8
TE = NI * NK
VMEM_BIG = 56 * 1024 * 1024

NT_DIMS = (((1,), (1,)), ((), ()))


def _dot(a, b):
    return jnp.dot(a, b, preferred_element_type=F32)


def _dot_nt(a, b):
    return lax.dot_general(a, b, NT_DIMS, preferred_element_type=F32)


def _rms(x, g):
    return x * lax.rsqrt(jnp.mean(x * x, axis=-1, keepdims=True) + EPS) * g


def _group_rows(ref, tile, n_prompt_tiles, groups):
    s = jnp.maximum(tile - n_prompt_tiles, 0)
    rows_s = ref[pl.ds(pl.multiple_of(s * groups, groups), groups), :]
    rows_p = jnp.broadcast_to(ref[PROMPT_ROW:PROMPT_ROW + 1, :], rows_s.shape)
    is_p = jnp.full(rows_s.shape, tile, jnp.int32) < n_prompt_tiles
    return jnp.where(is_p, rows_p, rows_s)


def _per_group(x, fn, *rows):
    n, d = x.shape
    g = rows[0].shape[0]
    x3 = x.reshape(g, n // g, d)
    return fn(x3, *[r[:, None, :] for r in rows]).reshape(n, d)


def _select_tile(tile, n_prompt_tiles, p_ref, s_ref):
    vp = p_ref[...]
    vs = s_ref[...]
    is_p = jnp.full(vp.shape, tile, jnp.int32) < n_prompt_tiles
    return jnp.where(is_p, vp, vs)


def _const(shape):
    nd = len(shape)
    return pl.BlockSpec(shape, lambda *_: (0,) * nd, pipeline_mode=pl.Buffered(1))


def _params(sem, vmem=None):
    return pltpu.CompilerParams(dimension_semantics=sem, vmem_limit_bytes=vmem)


ADA_TN = 1536


def _ada_kernel(c_ref, w_ref, b_ref, o_ref):
    o_ref[...] = _dot(c_ref[...].astype(BF16), w_ref[...].astype(BF16)) + b_ref[...]


def _ada(c_all, w_ada, b_ada):
    n = w_ada.shape[1]
    return pl.pallas_call(
        _ada_kernel,
        out_shape=jax.ShapeDtypeStruct((MODROWS, n), F32),
        grid=(n // ADA_TN,),
        in_specs=[pl.BlockSpec((MODROWS, D), lambda j: (0, 0)),
                  pl.BlockSpec((D, ADA_TN), lambda j: (0, j)),
                  pl.BlockSpec((1, ADA_TN), lambda j: (0, j))],
        out_specs=pl.BlockSpec((MODROWS, ADA_TN), lambda j: (0, j)),
        compiler_params=_params(("arbitrary",), VMEM_BIG),
        name="ada",
    )(c_all, w_ada, b_ada)


def _mod_spec(k):
    return pl.BlockSpec((MODROWS, D), lambda *_: (0, k))


def _tok(width):
    return pl.BlockSpec((TM, width), lambda i: (i, 0))


def _tok_p(width):
    return pl.BlockSpec((TM, width), lambda i: (jnp.minimum(i, NPT - 1), 0))


def _tok_s(width):
    return pl.BlockSpec((TM, width), lambda i: (jnp.maximum(i - NPT, 0), 0))


def _lat_kernel(xp_ref, xs_ref, sh_ref, sc_ref, gn_ref, w_ref, gq_ref, gkv_ref, cos_ref, sin_ref,
                h_ref, cq_ref, ckv_ref, ckvb_ref, kr_ref):
    i = pl.program_id(0)
    x = _select_tile(i, NPT, xp_ref, xs_ref)
    xn = _rms(x, gn_ref[...])
    sh = _group_rows(sh_ref, i, NPT, TM // SS)
    sc = _group_rows(sc_ref, i, NPT, TM // SS)
    hb = _per_group(xn, lambda a, s, c: a * (1 + c) + s, sh, sc).astype(BF16)
    h_ref[...] = hb
    p = _dot(hb, w_ref[...])
    cq_ref[...] = _rms(p[:, :QL], gq_ref[...]).astype(BF16)
    ckv = _rms(p[:, QL:QL + KVL], gkv_ref[...])
    ckv_ref[...] = ckv
    ckvb_ref[...] = ckv.astype(BF16)
    o = QL + KVL
    kr_ref[...] = p[:, o:o + DR] * cos_ref[...] + p[:, o + DR:o + 2 * DR] * sin_ref[...]


def _lat(xp, xs, mod, g_n1, w_lat, g_q, g_kv, cos2, sin2):
    wl = w_lat.shape[1]
    return pl.pallas_call(
        _lat_kernel,
        out_shape=(jax.ShapeDtypeStruct((T, D), BF16),
                   jax.ShapeDtypeStruct((T, QL), BF16),
                   jax.ShapeDtypeStruct((T, KVL), F32),
                   jax.ShapeDtypeStruct((T, KVL), BF16),
                   jax.ShapeDtypeStruct((T, DR), F32)),
        grid=(NT_,),
        in_specs=[_tok_p(D), _tok_s(D), _mod_spec(0), _mod_spec(1), _const((1, D)),
                  _const((D, wl)), _const((1, QL)), _const((1, KVL)), _tok(DR), _tok(DR)],
        out_specs=(_tok(D), _tok(QL), _tok(KVL), _tok(KVL), _tok(DR)),
        compiler_params=_params(("arbitrary",), VMEM_BIG),
        name="lat",
    )(xp, xs, mod, mod, g_n1, w_lat, g_q, g_kv, cos2, sin2)


def _conv_kernel(h_ref, w_ref, wc_ref, bc_ref, s1_ref, s2_ref, u_ref, zs_ref, zp_ref, carry_ref):
    i = pl.program_id(0)

    @pl.when(i == 0)
    def _():
        carry_ref[...] = jnp.zeros_like(carry_ref)

    p = _dot(h_ref[...], w_ref[...])
    z = p[:, 2 * CW:] * p[:, :CW]
    pb = p[:, CW:2 * CW]
    row = lax.broadcasted_iota(jnp.int32, (TM, CW), 0)
    pos = row & (SS - 1)
    is_p = jnp.full((TM, CW), i, jnp.int32) < NPT
    c6 = jnp.broadcast_to(carry_ref[6:7, :], (TM, CW))
    c7 = jnp.broadcast_to(carry_ref[7:8, :], (TM, CW))
    left = jnp.where(is_p, row, pos)
    m1 = left == 0
    m2 = left < 2
    ov1 = jnp.where(is_p, c7, s1_ref[...])
    ov2 = jnp.where(is_p, jnp.where(row == 0, c6, c7), s2_ref[...])
    z1 = jnp.where(m1, ov1, pltpu.roll(z, 1, 0))
    z2 = jnp.where(m2, ov2, pltpu.roll(z, 2, 0))
    yc = wc_ref[0:1, :] * z2 + wc_ref[1:2, :] * z1 + wc_ref[2:3, :] * z + bc_ref[...]
    u_ref[...] = (pb * yc).astype(BF16)
    zs_ref[...] = z
    tail = z[TM - 8:, :]

    @pl.when(i < NPT)
    def _():
        zp_ref[...] = tail

    carry_ref[...] = tail


def _conv(h, w_hbc, w_conv, b_conv, s1, s2):
    return pl.pallas_call(
        _conv_kernel,
        out_shape=(jax.ShapeDtypeStruct((T, CW), BF16),
                   jax.ShapeDtypeStruct((TS, CW), F32),
                   jax.ShapeDtypeStruct((8, CW), F32)),
        grid=(NT_,),
        in_specs=[_tok(D), _const((D, 3 * CW)), _const((3, CW)), _const((1, CW)),
                  _tok_s(CW), _tok_s(CW)],
        out_specs=(_tok(CW), _tok_s(CW), pl.BlockSpec((8, CW), lambda i: (0, 0))),
        scratch_shapes=[pltpu.VMEM((8, CW), F32)],
        compiler_params=_params(("arbitrary",), VMEM_BIG),
        name="conv",
    )(h, w_hbc, w_conv, b_conv, s1, s2)


def _gate_kernel(h_ref, w_ref, g_ref):
    g_ref[...] = jax.nn.sigmoid(_dot(h_ref[...], w_ref[...])).astype(BF16)


def _gate(h, w_g):
    n = w_g.shape[1]
    return pl.pallas_call(
        _gate_kernel,
        out_shape=jax.ShapeDtypeStruct((T, n), BF16),
        grid=(NT_,),
        in_specs=[_tok(D), _const((D, n))],
        out_specs=_tok(n),
        compiler_params=_params(("arbitrary",), VMEM_BIG),
        name="gate",
    )(h, w_g)


QW = DN + 2 * DR


def _qkv_kernel(cq_ref, ckv_ref, kr_ref, cos_ref, sin_ref, wq_ref, wuk_ref, wuv_ref,
                q_ref, k_ref, v_ref):
    qf = _dot(cq_ref[...], wq_ref[...])
    kf = _dot(ckv_ref[...], wuk_ref[...])
    vf = _dot(ckv_ref[...], wuv_ref[...])
    cos = cos_ref[...]
    sin = sin_ref[...]
    krb = kr_ref[...].astype(BF16)
    for h in range(NH):
        o = h * QW
        q_ref[h, :, 0:DN] = qf[:, o:o + DN].astype(BF16)
        q_ref[h, :, DN:DQ] = (qf[:, o + DN:o + DN + DR] * cos
                              + qf[:, o + DN + DR:o + QW] * sin).astype(BF16)
        k_ref[h, :, 0:DN] = kf[:, h * DN:(h + 1) * DN].astype(BF16)
        k_ref[h, :, DN:DQ] = krb
        v_ref[h] = vf[:, h * DV:(h + 1) * DV].astype(BF16)


def _qkv(cq, ckvb, kr, cos2, sin2, w_q, w_uk, w_uv):
    hs = lambda w: pl.BlockSpec((NH, TM, w), lambda i: (0, i, 0))
    return pl.pallas_call(
        _qkv_kernel,
        out_shape=(jax.ShapeDtypeStruct((NH, T, DQ), BF16),
                   jax.ShapeDtypeStruct((NH, T, DQ), BF16),
                   jax.ShapeDtypeStruct((NH, T, DV), BF16)),
        grid=(NT_,),
        in_specs=[_tok(QL), _tok(KVL), _tok(DR), _tok(DR), _tok(DR),
                  _const((QL, NH * QW)), _const((KVL, NH * DN)), _const((KVL, NH * DV))],
        out_specs=(hs(DQ), hs(DQ), hs(DV)),
        compiler_params=_params(("arbitrary",), VMEM_BIG),
        name="qkv",
    )(cq, ckvb, kr, cos2, sin2, w_q, w_uk, w_uv)


def _attn_kernel(q_ref, k_ref, v_ref, o_ref):
    qi = pl.program_id(1)
    q = q_ref[0]

    def step(off, carry, mask):
        m, l, acc = carry
        k = k_ref[0, pl.ds(off, TK), :]
        v = v_ref[0, pl.ds(off, TK), :]
        s = _dot_nt(q, k) * SCALE
        if mask is not None:
            s = jnp.where(mask, s, NEG)
        m_new = jnp.maximum(m, jnp.max(s, axis=-1, keepdims=True))
        alpha = jnp.exp(m - m_new)
        p = jnp.exp(s - m_new)
        l = alpha * l + jnp.sum(p, axis=-1, keepdims=True)
        acc = alpha * acc + _dot(p.astype(BF16), v)
        return m_new, l, acc

    init = (jnp.full((TQ, 1), NEG, F32), jnp.zeros((TQ, 1), F32), jnp.zeros((TQ, DV), F32))
    carry = lax.fori_loop(
        0, qi * (TQ // TK), lambda j, c: step(pl.multiple_of(j * TK, TK), c, None), init)
    row = lax.broadcasted_iota(jnp.int32, (TQ, TK), 0)
    col = lax.broadcasted_iota(jnp.int32, (TQ, TK), 1)
    for d in range(TQ // TK):
        mask = ((col + d * TK) // CHUNK) <= (row // CHUNK)
        carry = step(pl.multiple_of(qi * TQ + d * TK, TK), carry, mask)
    m, l, acc = carry
    o_ref[...] = (acc / l).astype(BF16)


def _attn(q, k, v):
    return pl.pallas_call(
        _attn_kernel,
        out_shape=jax.ShapeDtypeStruct((TP, NH * DV), BF16),
        grid=(NH, TP // TQ),
        in_specs=[pl.BlockSpec((1, TQ, DQ), lambda h, i: (h, i, 0)),
                  pl.BlockSpec((1, TP, DQ), lambda h, i: (h, 0, 0)),
                  pl.BlockSpec((1, TP, DV), lambda h, i: (h, 0, 0))],
        out_specs=pl.BlockSpec((TQ, DV), lambda h, i: (i, h)),
        compiler_params=_params(("arbitrary", "arbitrary"), VMEM_BIG),
        name="attn",
    )(q, k, v)


def _sq_kernel(q_ref, w_ref, o_ref):
    o_ref[0] = _dot(q_ref[0, :, 0:DN], w_ref[0]).astype(BF16)


def _sq(q, w_ukt):
    return pl.pallas_call(
        _sq_kernel,
        out_shape=jax.ShapeDtypeStruct((NH, TS, KVL), BF16),
        grid=(NH,),
        in_specs=[pl.BlockSpec((1, TS, DQ), lambda h: (h, TP // TS, 0)),
                  pl.BlockSpec((1, DN, KVL), lambda h: (h, 0, 0))],
        out_specs=pl.BlockSpec((1, TS, KVL), lambda h: (h, 0, 0)),
        compiler_params=_params(("arbitrary",)),
        name="sq",
    )(q, w_ukt)


def _sattn_kernel(qa_ref, q_ref, cc_ref, ck_ref, nc_ref, nk_ref, o_ref):
    rows = NH * SS
    qa = qa_ref[...].reshape(rows, KVL)
    qr = q_ref[:, :, DN:DQ].reshape(rows, DR)
    cc = cc_ref[0, 0].astype(BF16)
    ck = ck_ref[0, 0].astype(BF16)
    nc = nc_ref[...]
    nk = nk_ref[...].astype(BF16)
    s_c = (_dot_nt(qa, cc) + _dot_nt(qr, ck)) * SCALE
    s_n = (_dot_nt(qa, nc) + _dot_nt(qr, nk)) * SCALE
    qchunk_c = (PAST + (lax.broadcasted_iota(jnp.int32, (rows, PAST), 0) & (SS - 1))) // CHUNK
    qchunk_n = (PAST + (lax.broadcasted_iota(jnp.int32, (rows, SS), 0) & (SS - 1))) // CHUNK
    kchunk_c = lax.broadcasted_iota(jnp.int32, (rows, PAST), 1) // CHUNK
    kchunk_n = (PAST + lax.broadcasted_iota(jnp.int32, (rows, SS), 1)) // CHUNK
    s_c = jnp.where(kchunk_c <= qchunk_c, s_c, NEG)
    s_n = jnp.where(kchunk_n <= qchunk_n, s_n, NEG)
    m = jnp.maximum(jnp.max(s_c, axis=-1, keepdims=True), jnp.max(s_n, axis=-1, keepdims=True))
    p_c = jnp.exp(s_c - m)
    p_n = jnp.exp(s_n - m)
    l = jnp.sum(p_c, axis=-1, keepdims=True) + jnp.sum(p_n, axis=-1, keepdims=True)
    o = (_dot(p_c.astype(BF16), cc) + _dot(p_n.astype(BF16), nc)) / l
    o_ref[...] = o.astype(BF16).reshape(NH, SS, KVL)


def _sattn(q_abs, q, cache_ckv, cache_krope, ckvb, kr):
    nb0 = TP // SS
    return pl.pallas_call(
        _sattn_kernel,
        out_shape=jax.ShapeDtypeStruct((NH, TS, KVL), BF16),
        grid=(NB,),
        in_specs=[pl.BlockSpec((NH, SS, KVL), lambda b: (0, b, 0)),
                  pl.BlockSpec((NH, SS, DQ), lambda b: (0, nb0 + b, 0)),
                  pl.BlockSpec((1, 1, PAST, KVL), lambda b: (0, b, 0, 0)),
                  pl.BlockSpec((1, 1, PAST, DR), lambda b: (0, b, 0, 0)),
                  pl.BlockSpec((SS, KVL), lambda b: (nb0 + b, 0)),
                  pl.BlockSpec((SS, DR), lambda b: (nb0 + b, 0))],
        out_specs=pl.BlockSpec((NH, SS, KVL), lambda b: (0, b, 0)),
        compiler_params=_params(("arbitrary",)),
        name="sattn",
    )(q_abs, q, cache_ckv, cache_krope, ckvb, kr)


def _so_kernel(ol_ref, w_ref, o_ref):
    o_ref[...] = _dot(ol_ref[0], w_ref[...]).astype(BF16)


def _so(o_lat, w_uv):
    return pl.pallas_call(
        _so_kernel,
        out_shape=jax.ShapeDtypeStruct((TS, NH * DV), BF16),
        grid=(NH,),
        in_specs=[pl.BlockSpec((1, TS, KVL), lambda h: (h, 0, 0)),
                  pl.BlockSpec((KVL, DV), lambda h: (0, h))],
        out_specs=pl.BlockSpec((TS, DV), lambda h: (0, h)),
        compiler_params=_params(("arbitrary",)),
        name="so",
    )(o_lat, w_uv)


def _post_kernel(op_ref, os_ref, u_ref, g_ref, xp_ref, xs_ref, gt_ref, sh_ref, sc_ref, gn_ref,
                 woa_ref, wob_ref, wo_ref, wpq_ref, x1_ref, qp_ref):
    i = pl.program_id(0)
    o = _select_tile(i, NPT, op_ref, os_ref)
    x = _select_tile(i, NPT, xp_ref, xs_ref)
    a = _dot(o, woa_ref[...])
    b = _dot(u_ref[...], wob_ref[...])
    merged = g_ref[:, :D].astype(F32) * a + g_ref[:, D:].astype(F32) * b
    y = _dot(merged.astype(BF16), wo_ref[...])
    groups = TM // SS
    gt = _group_rows(gt_ref, i, NPT, groups)
    x1 = _per_group(y, lambda yy, g: g * yy, gt) + x
    x1_ref[...] = x1
    sh = _group_rows(sh_ref, i, NPT, groups)
    sc = _group_rows(sc_ref, i, NPT, groups)
    h2 = _per_group(_rms(x1, gn_ref[...]), lambda aa, s, c: aa * (1 + c) + s, sh, sc)
    qf = _dot(h2.astype(BF16), wpq_ref[...])
    for c in range(2 * PH):
        qp_ref[c] = qf[:, c * NK:(c + 1) * NK].astype(BF16)


def _post(o_p, o_s, u, g, xp, xs, mod, g_n2, w_oa, w_ob, w_o, w_pq):
    return pl.pallas_call(
        _post_kernel,
        out_shape=(jax.ShapeDtypeStruct((T, D), F32),
                   jax.ShapeDtypeStruct((2 * PH, T, NK), BF16)),
        grid=(NT_,),
        in_specs=[_tok_p(NH * DV), _tok_s(NH * DV), _tok(CW), _tok(2 * D), _tok_p(D), _tok_s(D),
                  _mod_spec(2), _mod_spec(3), _mod_spec(4), _const((1, D)),
                  _const((NH * DV, D)), _const((CW, D)), _const((D, D)), _const((D, D))],
        out_specs=(_tok(D), pl.BlockSpec((2 * PH, TM, NK), lambda i: (0, i, 0))),
        compiler_params=_params(("arbitrary",), VMEM_BIG),
        name="post",
    )(o_p, o_s, u, g, xp, xs, mod, mod, mod, g_n2, w_oa, w_ob, w_o, w_pq)


LG = 128


def _topk_rank(s):
    iota = lax.broadcasted_iota(jnp.int32, s.shape, 0).astype(F32)
    iota16 = lax.broadcasted_iota(jnp.int32, (TOPK, s.shape[1]), 0)
    rank = jnp.full(s.shape, float(TOPK), F32)
    vals = jnp.zeros((TOPK, s.shape[1]), F32)
    for k in range(TOPK):
        m = jnp.max(s, axis=0, keepdims=True)
        idx = jnp.min(jnp.where(s == m, iota, float(NK)), axis=0, keepdims=True)
        hit = iota == idx
        rank = jnp.where(hit, float(k), rank)
        s = jnp.where(hit, -jnp.inf, s)
        vals = jnp.where(iota16 == k, m, vals)
    return vals, rank


def _pair_counts(v1, v2):
    n = v1.shape[1]
    i16 = lax.broadcasted_iota(jnp.int32, (TOPK, n), 0).astype(F32)
    i8 = lax.broadcasted_iota(jnp.int32, (8, n), 0).astype(F32)
    blocks = [v1 + v2[0:1, :]]
    idxs = [i16 * TOPK]
    for b in range(1, 8):
        blocks.append(v1[0:8, :] + v2[b:b + 1, :])
        idxs.append(i8 * TOPK + b)
    blocks.append(v1[0:1, :] + v2[8:16, :])
    idxs.append(i8 + 8.0)
    c = jnp.concatenate(blocks, axis=0)
    ci = jnp.concatenate(idxs, axis=0)
    counts = jnp.zeros((TOPK, n), F32)
    m0 = None
    z = None
    for k in range(TOPK):
        m = jnp.max(c, axis=0, keepdims=True)
        if k == 0:
            m0 = m
            z = jnp.ones_like(m)
        else:
            z = z + jnp.exp(m - m0)
        idx = jnp.min(jnp.where(c == m, ci, float(TOPK * TOPK)), axis=0, keepdims=True)
        c = jnp.where(ci == idx, -jnp.inf, c)
        a_sel = jnp.floor(idx * (1.0 / TOPK))
        counts = counts + jnp.where(i16 == a_sel, 1.0, 0.0)
    return counts, z


def _route_math(s1, s2):
    v1, r1 = _topk_rank(s1)
    v2, r2 = _topk_rank(s2)
    counts, z = _pair_counts(v1, v2)
    lim = jnp.zeros_like(s1)
    for a in range(TOPK):
        lim = lim + jnp.where(r1 == float(a), counts[a:a + 1, :], 0.0)
    e1 = jnp.exp(s1 - v1[0:1, :])
    e2 = jnp.exp(s2 - v2[0:1, :]) / z
    return r2, e2, lim, e1


def _route_kernel(qp_ref, k1_ref, k2_ref, r2_ref, e2_ref, lim_ref, e1_ref):
    def head(h, _):
        s1 = _dot_nt(k1_ref[h], qp_ref[2 * h])
        s2 = _dot_nt(k2_ref[h], qp_ref[2 * h + 1])
        for g in range(TM // LG):
            sl = slice(g * LG, (g + 1) * LG)
            r2, e2, lim, e1 = _route_math(s1[:, sl], s2[:, sl])
            r2_ref[h, :, sl] = r2.astype(BF16)
            e2_ref[h, :, sl] = e2.astype(BF16)
            lim_ref[h, :, sl] = lim
            e1_ref[h, :, sl] = e1
        return 0

    lax.fori_loop(0, PH, head, 0)


def _route(qp, k1, k2):
    rs = pl.BlockSpec((PH, NK, TM), lambda i: (0, 0, i))
    return pl.pallas_call(
        _route_kernel,
        out_shape=(jax.ShapeDtypeStruct((PH, NK, T), BF16),
                   jax.ShapeDtypeStruct((PH, NK, T), BF16),
                   jax.ShapeDtypeStruct((PH, NK, T), F32),
                   jax.ShapeDtypeStruct((PH, NK, T), F32)),
        grid=(NT_,),
        in_specs=[pl.BlockSpec((2 * PH, TM, NK), lambda i: (0, i, 0)),
                  _const((PH, NK, NK)), _const((PH, NK, NK))],
        out_specs=(rs, rs, rs, rs),
        compiler_params=_params(("arbitrary",), VMEM_BIG),
        name="route",
    )(qp, k1, k2)


SQRT_HALF = 0.7071067811865476


def _gelu(a):
    return 0.5 * a * (1.0 + lax.erf(a * SQRT_HALF))


def _peer_kernel(x1_ref, sh_ref, sc_ref, gn_ref, u_ref, vt_ref, r2_ref, e2_ref, lim_ref, e1_ref,
                 o_ref, h2_ref, at_ref, wt_ref, acc_ref):
    ti = pl.program_id(0)
    ei = pl.program_id(1)

    @pl.when(ei == 0)
    def _():
        groups = TMP // SS
        sh = _group_rows(sh_ref, ti, TP // TMP, groups)
        sc = _group_rows(sc_ref, ti, TP // TMP, groups)
        h2 = _per_group(_rms(x1_ref[...], gn_ref[...]), lambda aa, s, c: aa * (1 + c) + s, sh, sc)
        h2_ref[...] = h2.astype(BF16)
        acc_ref[...] = jnp.zeros_like(acc_ref)

    at_ref[...] = _dot_nt(u_ref[...], h2_ref[...])

    for il in range(NI):
        rows = slice(il * NK, (il + 1) * NK)
        for g in range(TMP // LG):
            sl = slice(g * LG, (g + 1) * LG)
            act = _gelu(at_ref[rows, sl]).astype(BF16)
            gate = jnp.zeros((NK, LG), BF16)
            for h in range(PH):
                lim = lim_ref[h, il:il + 1, sl].astype(BF16)
                e1 = e1_ref[h, il:il + 1, sl].astype(BF16)
                gate = gate + jnp.where(r2_ref[h, :, sl] < lim, e2_ref[h, :, sl], 0) * e1
            wt_ref[rows, sl] = gate * act
    acc_ref[...] += _dot(vt_ref[...], wt_ref[...])

    @pl.when(ei == pl.num_programs(1) - 1)
    def _():
        for c in range(D // LG):
            o_ref[:, c * LG:(c + 1) * LG] = acc_ref[c * LG:(c + 1) * LG, :].T


def _peer(x1, mod, g_n2, w_u, w_vt, r2, e2, lim, e1):
    return pl.pallas_call(
        _peer_kernel,
        out_shape=jax.ShapeDtypeStruct((T, D), F32),
        grid=(T // TMP, NK // NI),
        in_specs=[pl.BlockSpec((TMP, D), lambda t, e: (t, 0)),
                  _mod_spec(3), _mod_spec(4), _const((1, D)),
                  pl.BlockSpec((TE, D), lambda t, e: (e, 0)),
                  pl.BlockSpec((D, TE), lambda t, e: (0, e)),
                  pl.BlockSpec((PH, NK, TMP), lambda t, e: (0, 0, t)),
                  pl.BlockSpec((PH, NK, TMP), lambda t, e: (0, 0, t)),
                  pl.BlockSpec((PH, NI, TMP), lambda t, e: (0, e, t)),
                  pl.BlockSpec((PH, NI, TMP), lambda t, e: (0, e, t))],
        out_specs=pl.BlockSpec((TMP, D), lambda t, e: (t, 0)),
        scratch_shapes=[pltpu.VMEM((TMP, D), BF16), pltpu.VMEM((TE, TMP), F32),
                        pltpu.VMEM((TE, TMP), BF16), pltpu.VMEM((D, TMP), F32)],
        compiler_params=_params(("arbitrary", "arbitrary"), VMEM_BIG),
        name="peer",
    )(x1, mod, mod, g_n2, w_u, w_vt, r2, e2, lim, e1)


def _final_kernel(x1_ref, pe_ref, gt_ref, gf_ref, yp_ref, ys_ref):
    i = pl.program_id(0)
    gt = _group_rows(gt_ref, i, NPT, TM // SS)
    x2 = x1_ref[...] + _per_group(pe_ref[...], lambda pp, g: g * pp, gt)
    y = _rms(x2, gf_ref[...])

    @pl.when(i < NPT)
    def _():
        yp_ref[...] = y

    ys_ref[...] = y


def _final(x1, pe, mod, g_f):
    return pl.pallas_call(
        _final_kernel,
        out_shape=(jax.ShapeDtypeStruct((TP, D), F32), jax.ShapeDtypeStruct((TS, D), F32)),
        grid=(NT_,),
        in_specs=[_tok(D), _tok(D), _mod_spec(5), _const((1, D))],
        out_specs=(_tok_p(D), _tok_s(D)),
        compiler_params=_params(("arbitrary",), VMEM_BIG),
        name="final",
    )(x1, pe, mod, g_f)


def _rot_cols(w):
    half = w.shape[-1] // 2
    return jnp.concatenate([-w[..., half:], w[..., :half]], axis=-1)


def _rope_tables():
    half = DR // 2
    pos = jnp.concatenate([jnp.arange(TP), PAST + jnp.tile(jnp.arange(SS), NB)])
    inv = 1.0 / (ROPE_THETA ** (jnp.arange(half, dtype=F32) / half))
    ang = pos.astype(F32)[:, None] * inv[None, :]
    cos = jnp.cos(ang)
    sin = jnp.sin(ang)
    return jnp.concatenate([cos, cos], axis=1), jnp.concatenate([sin, sin], axis=1)


def kernel(x_prompt, x_sample, cache_ckv, cache_krope, state_conv, c_prompt, c_sample, w_ada, b_ada, g_n1, w_in, g_q, g_kv, w_uq, w_uk, w_uv, w_oa, w_conv, b_conv, w_ob, w_o, g_n2, w_pq, sub_k1, sub_k2, w_u, w_v, g_f):
    assert x_prompt.shape == (1, TP, D) and x_sample.shape == (NB, SS, D)
    assert cache_ckv.shape == (1, NB, PAST, KVL) and w_u.shape == (1, NE, D)
    xp = x_prompt.reshape(TP, D)
    xs = x_sample.reshape(TS, D)
    c_all = jnp.concatenate([c_sample, c_prompt, jnp.zeros((MODROWS - NB - 1, D), F32)], axis=0)
    cos2, sin2 = _rope_tables()

    w = w_in[0]
    o_kr = QL + KVL
    o_h = o_kr + DR
    w_lat = jnp.concatenate([w[:, :o_h], _rot_cols(w[:, o_kr:o_h])], axis=1).astype(BF16)
    w_hbc = w[:, o_h:o_h + 3 * CW].astype(BF16)
    w_g = w[:, o_h + 3 * CW:].astype(BF16)
    wq = w_uq[0]
    w_q = jnp.concatenate([wq, _rot_cols(wq[..., DN:])], axis=-1).reshape(QL, NH * QW).astype(BF16)
    w_uk2 = w_uk[0].reshape(KVL, NH * DN).astype(BF16)
    w_ukt = jnp.transpose(w_uk[0], (1, 2, 0)).astype(BF16)
    w_uv2 = w_uv[0].reshape(KVL, NH * DV).astype(BF16)
    state = state_conv[0]
    s1 = jnp.pad(state[:, 1:2], ((0, 0), (0, SS - 1), (0, 0))).reshape(TS, CW)
    s2 = jnp.pad(state, ((0, 0), (0, SS - 2), (0, 0))).reshape(TS, CW)

    mod = _ada(c_all, w_ada[0], b_ada)
    h, cq, ckv, ckvb, kr = _lat(xp, xs, mod, g_n1, w_lat, g_q, g_kv, cos2, sin2)
    u, zs, zp = _conv(h, w_hbc, w_conv[0], b_conv, s1, s2)
    g = _gate(h, w_g)
    q, k, v = _qkv(cq, ckvb, kr, cos2, sin2, w_q, w_uk2, w_uv2)
    o_p = _attn(q, k, v)
    q_abs = _sq(q, w_ukt)
    o_lat = _sattn(q_abs, q, cache_ckv, cache_krope, ckvb, kr)
    o_s = _so(o_lat, w_uv2)
    x1, qp = _post(o_p, o_s, u, g, xp, xs, mod, g_n2, w_oa[0].astype(BF16), w_ob[0].astype(BF16),
                   w_o[0].astype(BF16), w_pq[0].astype(BF16))
    r2, e2, lim, e1 = _route(qp, sub_k1[0].astype(BF16), sub_k2[0].astype(BF16))
    pe = _peer(x1, mod, g_n2, w_u[0].astype(BF16), jnp.transpose(w_v[0]).astype(BF16),
               r2, e2, lim, e1)
    y_p, y_s = _final(x1, pe, mod, g_f.reshape(1, D))

    return (y_p.reshape(1, TP, D), y_s.reshape(NB, SS, D),
            ckv[:TP].reshape(1, 1, TP, KVL), kr[:TP].reshape(1, 1, TP, DR),
            zp[6:8].reshape(1, 1, 2, CW),
            ckv[TP:].reshape(1, NB, SS, KVL), kr[TP:].reshape(1, NB, SS, DR),
            zs.reshape(NB, SS, CW)[:, SS - 2:].reshape(1, NB, 2, CW))
```

```python
import functools

import jax
import jax.numpy as jnp
from jax import lax
from jax.experimental import pallas as pl
from jax.experimental.pallas import tpu as pltpu

F32 = jnp.float32
BF16 = jnp.bfloat16

D = 2048
TP = 8192
NB = 32
SS = 16
TS = NB * SS
T = TP + TS
PAST = 1024
CHUNK = 64
NH = 8
DN = 128
DR = 64
DQ = DN + DR
DV = 128
QL = 512
KVL = 512
ROPE_THETA = 10000.0
SCALE = (DN + DR) ** -0.5
CW = 1024
PH = 8
NK = 128
NE = NK * NK
TOPK = 16
EPS = 1e-6
NEG = float(jnp.finfo(jnp.float32).min)

TM = 256
NPT = TP // TM
NT_ = T // TM
MODROWS = 40
PROMPT_ROW = NB

TMP = 512
NI = 4
TE = NI * NK
NPAIR = NK // (2 * NI)
VMEM_BIG = 56 * 1024 * 1024

NT_DIMS = (((1,), (1,)), ((), ()))


def _dot(a, b):
    return jnp.dot(a, b, preferred_element_type=F32)


def _dot_nt(a, b):
    return lax.dot_general(a, b, NT_DIMS, preferred_element_type=F32)


def _rms(x, g):
    return x * lax.rsqrt(jnp.mean(x * x, axis=-1, keepdims=True) + EPS) * g


def _group_rows(ref, tile, n_prompt_tiles, groups):
    s = jnp.maximum(tile - n_prompt_tiles, 0)
    rows_s = ref[pl.ds(pl.multiple_of(s * groups, groups), groups), :]
    rows_p = jnp.broadcast_to(ref[PROMPT_ROW:PROMPT_ROW + 1, :], rows_s.shape)
    is_p = jnp.full(rows_s.shape, tile, jnp.int32) < n_prompt_tiles
    return jnp.where(is_p, rows_p, rows_s)


def _per_group(x, fn, *rows):
    n, d = x.shape
    g = rows[0].shape[0]
    x3 = x.reshape(g, n // g, d)
    return fn(x3, *[r[:, None, :] for r in rows]).reshape(n, d)


def _select_tile(tile, n_prompt_tiles, p_ref, s_ref):
    vp = p_ref[...]
    vs = s_ref[...]
    is_p = jnp.full(vp.shape, tile, jnp.int32) < n_prompt_tiles
    return jnp.where(is_p, vp, vs)


def _const(shape):
    nd = len(shape)
    return pl.BlockSpec(shape, lambda *_: (0,) * nd, pipeline_mode=pl.Buffered(1))


def _params(sem, vmem=None):
    return pltpu.CompilerParams(dimension_semantics=sem, vmem_limit_bytes=vmem)


ADA_TN = 1536


def _ada_kernel(c_ref, w_ref, b_ref, o_ref):
    o_ref[...] = _dot(c_ref[...].astype(BF16), w_ref[...].astype(BF16)) + b_ref[...]


def _ada(c_all, w_ada, b_ada):
    n = w_ada.shape[1]
    return pl.pallas_call(
        _ada_kernel,
        out_shape=jax.ShapeDtypeStruct((MODROWS, n), F32),
        grid=(n // ADA_TN,),
        in_specs=[pl.BlockSpec((MODROWS, D), lambda j: (0, 0)),
                  pl.BlockSpec((D, ADA_TN), lambda j: (0, j)),
                  pl.BlockSpec((1, ADA_TN), lambda j: (0, j))],
        out_specs=pl.BlockSpec((MODROWS, ADA_TN), lambda j: (0, j)),
        compiler_params=_params(("arbitrary",), VMEM_BIG),
        name="ada",
    )(c_all, w_ada, b_ada)


def _mod_spec(k):
    return pl.BlockSpec((MODROWS, D), lambda *_: (0, k))


def _tok(width):
    return pl.BlockSpec((TM, width), lambda i: (i, 0))


def _tok_p(width):
    return pl.BlockSpec((TM, width), lambda i: (jnp.minimum(i, NPT - 1), 0))


def _tok_s(width):
    return pl.BlockSpec((TM, width), lambda i: (jnp.maximum(i - NPT, 0), 0))


def _lat_kernel(xp_ref, xs_ref, sh_ref, sc_ref, gn_ref, w_ref, gq_ref, gkv_ref, cos_ref, sin_ref,
                h_ref, cq_ref, ckv_ref, ckvb_ref, kr_ref):
    i = pl.program_id(0)
    x = _select_tile(i, NPT, xp_ref, xs_ref)
    xn = _rms(x, gn_ref[...])
    sh = _group_rows(sh_ref, i, NPT, TM // SS)
    sc = _group_rows(sc_ref, i, NPT, TM // SS)
    hb = _per_group(xn, lambda a, s, c: a * (1 + c) + s, sh, sc).astype(BF16)
    h_ref[...] = hb
    p = _dot(hb, w_ref[...])
    cq_ref[...] = _rms(p[:, :QL], gq_ref[...]).astype(BF16)
    ckv = _rms(p[:, QL:QL + KVL], gkv_ref[...])
    ckv_ref[...] = ckv
    ckvb_ref[...] = ckv.astype(BF16)
    o = QL + KVL
    kr_ref[...] = p[:, o:o + DR] * cos_ref[...] + p[:, o + DR:o + 2 * DR] * sin_ref[...]


def _lat(xp, xs, mod, g_n1, w_lat, g_q, g_kv, cos2, sin2):
    wl = w_lat.shape[1]
    return pl.pallas_call(
        _lat_kernel,
        out_shape=(jax.ShapeDtypeStruct((T, D), BF16),
                   jax.ShapeDtypeStruct((T, QL), BF16),
                   jax.ShapeDtypeStruct((T, KVL), F32),
                   jax.ShapeDtypeStruct((T, KVL), BF16),
                   jax.ShapeDtypeStruct((T, DR), F32)),
        grid=(NT_,),
        in_specs=[_tok_p(D), _tok_s(D), _mod_spec(0), _mod_spec(1), _const((1, D)),
                  _const((D, wl)), _const((1, QL)), _const((1, KVL)), _tok(DR), _tok(DR)],
        out_specs=(_tok(D), _tok(QL), _tok(KVL), _tok(KVL), _tok(DR)),
        compiler_params=_params(("arbitrary",), VMEM_BIG),
        name="lat",
    )(xp, xs, mod, mod, g_n1, w_lat, g_q, g_kv, cos2, sin2)


def _conv_kernel(h_ref, w_ref, wc_ref, bc_ref, s1_ref, s2_ref, u_ref, zs_ref, zp_ref, carry_ref):
    i = pl.program_id(0)

    @pl.when(i == 0)
    def _():
        carry_ref[...] = jnp.zeros_like(carry_ref)

    p = _dot(h_ref[...], w_ref[...])
    z = p[:, 2 * CW:] * p[:, :CW]
    pb = p[:, CW:2 * CW]
    row = lax.broadcasted_iota(jnp.int32, (TM, CW), 0)
    pos = row & (SS - 1)
    is_p = jnp.full((TM, CW), i, jnp.int32) < NPT
    c6 = jnp.broadcast_to(carry_ref[6:7, :], (TM, CW))
    c7 = jnp.broadcast_to(carry_ref[7:8, :], (TM, CW))
    left = jnp.where(is_p, row, pos)
    m1 = left == 0
    m2 = left < 2
    ov1 = jnp.where(is_p, c7, s1_ref[...])
    ov2 = jnp.where(is_p, jnp.where(row == 0, c6, c7), s2_ref[...])
    z1 = jnp.where(m1, ov1, pltpu.roll(z, 1, 0))
    z2 = jnp.where(m2, ov2, pltpu.roll(z, 2, 0))
    yc = wc_ref[0:1, :] * z2 + wc_ref[1:2, :] * z1 + wc_ref[2:3, :] * z + bc_ref[...]
    u_ref[...] = (pb * yc).astype(BF16)
    zs_ref[...] = z
    tail = z[TM - 8:, :]

    @pl.when(i < NPT)
    def _():
        zp_ref[...] = tail

    carry_ref[...] = tail


def _conv(h, w_hbc, w_conv, b_conv, s1, s2):
    return pl.pallas_call(
        _conv_kernel,
        out_shape=(jax.ShapeDtypeStruct((T, CW), BF16),
                   jax.ShapeDtypeStruct((TS, CW), F32),
                   jax.ShapeDtypeStruct((8, CW), F32)),
        grid=(NT_,),
        in_specs=[_tok(D), _const((D, 3 * CW)), _const((3, CW)), _const((1, CW)),
                  _tok_s(CW), _tok_s(CW)],
        out_specs=(_tok(CW), _tok_s(CW), pl.BlockSpec((8, CW), lambda i: (0, 0))),
        scratch_shapes=[pltpu.VMEM((8, CW), F32)],
        compiler_params=_params(("arbitrary",), VMEM_BIG),
        name="conv",
    )(h, w_hbc, w_conv, b_conv, s1, s2)


def _gate_kernel(h_ref, w_ref, g_ref):
    g_ref[...] = jax.nn.sigmoid(_dot(h_ref[...], w_ref[...])).astype(BF16)


def _gate(h, w_g):
    n = w_g.shape[1]
    return pl.pallas_call(
        _gate_kernel,
        out_shape=jax.ShapeDtypeStruct((T, n), BF16),
        grid=(NT_,),
        in_specs=[_tok(D), _const((D, n))],
        out_specs=_tok(n),
        compiler_params=_params(("arbitrary",), VMEM_BIG),
        name="gate",
    )(h, w_g)


QW = DN + 2 * DR


TMA = 512


def _qkv_kernel(cq_ref, ckv_ref, kr_ref, cos_ref, sin_ref, cost_ref, sint_ref,
                wq_ref, wqt_ref, wuk_ref, wuvt_ref, q_ref, qt_ref, k_ref, vt_ref):
    cq = cq_ref[...]
    ckv = ckv_ref[...]
    qf = _dot(cq, wq_ref[...])
    kf = _dot(ckv, wuk_ref[...])
    cos = cos_ref[...]
    sin = sin_ref[...]
    cost = cost_ref[...]
    sint = sint_ref[...]
    krb = kr_ref[...].astype(BF16)
    for h in range(NH):
        o = h * QW
        q_ref[h, :, 0:DN] = qf[:, o:o + DN].astype(BF16)
        q_ref[h, :, DN:DQ] = (qf[:, o + DN:o + DN + DR] * cos
                              + qf[:, o + DN + DR:o + QW] * sin).astype(BF16)
        qt = _dot_nt(wqt_ref[h], cq)
        qt_ref[h, 0:DN, :] = qt[0:DN, :].astype(BF16)
        qt_ref[h, DN:DQ, :] = (qt[DN:DN + DR, :] * cost + qt[DN + DR:QW, :] * sint).astype(BF16)
        k_ref[h, :, 0:DN] = kf[:, h * DN:(h + 1) * DN].astype(BF16)
        k_ref[h, :, DN:DQ] = krb
        vt_ref[h, 0] = _dot_nt(wuvt_ref[h], ckv).astype(BF16)


def _qkv(cq, ckvb, kr, cos2, sin2, cos2t, sin2t, w_q, w_qt, w_uk, w_uvt):
    tok = lambda w: pl.BlockSpec((TMA, w), lambda i: (i, 0))
    tokt = pl.BlockSpec((DR, TMA), lambda i: (0, i))
    return pl.pallas_call(
        _qkv_kernel,
        out_shape=(jax.ShapeDtypeStruct((NH, T, DQ), BF16),
                   jax.ShapeDtypeStruct((NH, DQ, T), BF16),
                   jax.ShapeDtypeStruct((NH, T, DQ), BF16),
                   jax.ShapeDtypeStruct((NH, T // TMA, DV, TMA), BF16)),
        grid=(T // TMA,),
        in_specs=[tok(QL), tok(KVL), tok(DR), tok(DR), tok(DR), tokt, tokt,
                  _const((QL, NH * QW)), _const((NH, QW, QL)), _const((KVL, NH * DN)),
                  _const((NH, DV, KVL))],
        out_specs=(pl.BlockSpec((NH, TMA, DQ), lambda i: (0, i, 0)),
                   pl.BlockSpec((NH, DQ, TMA), lambda i: (0, 0, i)),
                   pl.BlockSpec((NH, TMA, DQ), lambda i: (0, i, 0)),
                   pl.BlockSpec((NH, 1, DV, TMA), lambda i: (0, i, 0, 0))),
        compiler_params=_params(("arbitrary",), VMEM_BIG),
        name="qkv",
    )(cq, ckvb, kr, cos2, sin2, cos2t, sin2t, w_q, w_qt, w_uk, w_uvt)


EXP2_SCALE = SCALE * 1.4426950408889634


def _attn_kernel(qt_ref, k_ref, vt_ref, o_ref, m_ref, l_ref, acc_ref):
    qi = pl.program_id(1)
    qt = qt_ref[0]
    m_ref[...] = jnp.full_like(m_ref, NEG)
    l_ref[...] = jnp.zeros_like(l_ref)
    acc_ref[...] = jnp.zeros_like(acc_ref)

    def step(j, mask):
        k = k_ref[0, pl.ds(pl.multiple_of(j * TMA, TMA), TMA), :]
        s = _dot(k, qt)
        if mask is not None:
            s = jnp.where(mask, s, NEG)
        m = m_ref[...]
        m_new = jnp.maximum(m, jnp.max(s, axis=0, keepdims=True))
        alpha = jnp.exp2((m - m_new) * EXP2_SCALE)
        p = jnp.exp2((s - m_new) * EXP2_SCALE)
        l_ref[...] = alpha * l_ref[...] + jnp.sum(p, axis=0, keepdims=True)
        acc_ref[...] = alpha * acc_ref[...] + _dot(vt_ref[0, j], p.astype(BF16))
        m_ref[...] = m_new

    def body(j, c):
        step(j, None)
        return c

    lax.fori_loop(0, qi, body, 0)
    krow = lax.broadcasted_iota(jnp.int32, (TMA, TMA), 0)
    qcol = lax.broadcasted_iota(jnp.int32, (TMA, TMA), 1)
    step(qi, (krow // CHUNK) <= (qcol // CHUNK))
    o_ref[...] = (acc_ref[...] / l_ref[...]).T.astype(BF16)


def _attn(qt, k, vt):
    return pl.pallas_call(
        _attn_kernel,
        out_shape=jax.ShapeDtypeStruct((TP, NH * DV), BF16),
        grid=(NH, TP // TMA),
        in_specs=[pl.BlockSpec((1, DQ, TMA), lambda h, i: (h, 0, i)),
                  pl.BlockSpec((1, TP, DQ), lambda h, i: (h, 0, 0)),
                  pl.BlockSpec((1, TP // TMA, DV, TMA), lambda h, i: (h, 0, 0, 0))],
        out_specs=pl.BlockSpec((TMA, DV), lambda h, i: (i, h)),
        scratch_shapes=[pltpu.VMEM((1, TMA), F32), pltpu.VMEM((1, TMA), F32),
                        pltpu.VMEM((DV, TMA), F32)],
        compiler_params=_params(("arbitrary", "arbitrary"), VMEM_BIG),
        name="attn",
    )(qt, k, vt)


def _sq_kernel(q_ref, w_ref, o_ref):
    o_ref[0] = _dot(q_ref[0, :, 0:DN], w_ref[0]).astype(BF16)


def _sq(q, w_ukt):
    return pl.pallas_call(
        _sq_kernel,
        out_shape=jax.ShapeDtypeStruct((NH, TS, KVL), BF16),
        grid=(NH,),
        in_specs=[pl.BlockSpec((1, TS, DQ), lambda h: (h, TP // TS, 0)),
                  pl.BlockSpec((1, DN, KVL), lambda h: (h, 0, 0))],
        out_specs=pl.BlockSpec((1, TS, KVL), lambda h: (h, 0, 0)),
        compiler_params=_params(("arbitrary",)),
        name="sq",
    )(q, w_ukt)


def _sattn_kernel(qa_ref, q_ref, cc_ref, ck_ref, nc_ref, nk_ref, o_ref):
    rows = NH * SS
    qa = qa_ref[...].reshape(rows, KVL)
    qr = q_ref[:, :, DN:DQ].reshape(rows, DR)
    cc = cc_ref[0, 0].astype(BF16)
    ck = ck_ref[0, 0].astype(BF16)
    nc = nc_ref[...]
    nk = nk_ref[...].astype(BF16)
    s_c = (_dot_nt(qa, cc) + _dot_nt(qr, ck)) * SCALE
    s_n = (_dot_nt(qa, nc) + _dot_nt(qr, nk)) * SCALE
    qchunk_c = (PAST + (lax.broadcasted_iota(jnp.int32, (rows, PAST), 0) & (SS - 1))) // CHUNK
    qchunk_n = (PAST + (lax.broadcasted_iota(jnp.int32, (rows, SS), 0) & (SS - 1))) // CHUNK
    kchunk_c = lax.broadcasted_iota(jnp.int32, (rows, PAST), 1) // CHUNK
    kchunk_n = (PAST + lax.broadcasted_iota(jnp.int32, (rows, SS), 1)) // CHUNK
    s_c = jnp.where(kchunk_c <= qchunk_c, s_c, NEG)
    s_n = jnp.where(kchunk_n <= qchunk_n, s_n, NEG)
    m = jnp.maximum(jnp.max(s_c, axis=-1, keepdims=True), jnp.max(s_n, axis=-1, keepdims=True))
    p_c = jnp.exp(s_c - m)
    p_n = jnp.exp(s_n - m)
    l = jnp.sum(p_c, axis=-1, keepdims=True) + jnp.sum(p_n, axis=-1, keepdims=True)
    o = (_dot(p_c.astype(BF16), cc) + _dot(p_n.astype(BF16), nc)) / l
    o_ref[...] = o.astype(BF16).reshape(NH, SS, KVL)


def _sattn(q_abs, q, cache_ckv, cache_krope, ckvb, kr):
    nb0 = TP // SS
    return pl.pallas_call(
        _sattn_kernel,
        out_shape=jax.ShapeDtypeStruct((NH, TS, KVL), BF16),
        grid=(NB,),
        in_specs=[pl.BlockSpec((NH, SS, KVL), lambda b: (0, b, 0)),
                  pl.BlockSpec((NH, SS, DQ), lambda b: (0, nb0 + b, 0)),
                  pl.BlockSpec((1, 1, PAST, KVL), lambda b: (0, b, 0, 0)),
                  pl.BlockSpec((1, 1, PAST, DR), lambda b: (0, b, 0, 0)),
                  pl.BlockSpec((SS, KVL), lambda b: (nb0 + b, 0)),
                  pl.BlockSpec((SS, DR), lambda b: (nb0 + b, 0))],
        out_specs=pl.BlockSpec((NH, SS, KVL), lambda b: (0, b, 0)),
        compiler_params=_params(("arbitrary",)),
        name="sattn",
    )(q_abs, q, cache_ckv, cache_krope, ckvb, kr)


def _so_kernel(ol_ref, w_ref, o_ref):
    o_ref[...] = _dot(ol_ref[0], w_ref[...]).astype(BF16)


def _so(o_lat, w_uv):
    return pl.pallas_call(
        _so_kernel,
        out_shape=jax.ShapeDtypeStruct((TS, NH * DV), BF16),
        grid=(NH,),
        in_specs=[pl.BlockSpec((1, TS, KVL), lambda h: (h, 0, 0)),
                  pl.BlockSpec((KVL, DV), lambda h: (0, h))],
        out_specs=pl.BlockSpec((TS, DV), lambda h: (0, h)),
        compiler_params=_params(("arbitrary",)),
        name="so",
    )(o_lat, w_uv)


def _post_kernel(op_ref, os_ref, u_ref, g_ref, xp_ref, xs_ref, gt_ref, sh_ref, sc_ref, gn_ref,
                 woa_ref, wob_ref, wo_ref, wpq_ref, x1_ref, h2t_ref, qp_ref):
    i = pl.program_id(0)
    o = _select_tile(i, NPT, op_ref, os_ref)
    x = _select_tile(i, NPT, xp_ref, xs_ref)
    a = _dot(o, woa_ref[...])
    b = _dot(u_ref[...], wob_ref[...])
    merged = g_ref[:, :D].astype(F32) * a + g_ref[:, D:].astype(F32) * b
    y = _dot(merged.astype(BF16), wo_ref[...])
    groups = TM // SS
    gt = _group_rows(gt_ref, i, NPT, groups)
    x1 = _per_group(y, lambda yy, g: g * yy, gt) + x
    x1_ref[...] = x1
    sh = _group_rows(sh_ref, i, NPT, groups)
    sc = _group_rows(sc_ref, i, NPT, groups)
    h2 = _per_group(_rms(x1, gn_ref[...]), lambda aa, s, c: aa * (1 + c) + s, sh, sc)
    for c in range(D // LG):
        h2t_ref[c * LG:(c + 1) * LG, :] = h2[:, c * LG:(c + 1) * LG].T.astype(BF16)
    qf = _dot(h2.astype(BF16), wpq_ref[...])
    for c in range(2 * PH):
        qp_ref[c] = qf[:, c * NK:(c + 1) * NK].astype(BF16)


def _post(o_p, o_s, u, g, xp, xs, mod, g_n2, w_oa, w_ob, w_o, w_pq):
    return pl.pallas_call(
        _post_kernel,
        out_shape=(jax.ShapeDtypeStruct((T, D), F32),
                   jax.ShapeDtypeStruct((D, T), BF16),
                   jax.ShapeDtypeStruct((2 * PH, T, NK), BF16)),
        grid=(NT_,),
        in_specs=[_tok_p(NH * DV), _tok_s(NH * DV), _tok(CW), _tok(2 * D), _tok_p(D), _tok_s(D),
                  _mod_spec(2), _mod_spec(3), _mod_spec(4), _const((1, D)),
                  _const((NH * DV, D)), _const((CW, D)), _const((D, D)), _const((D, D))],
        out_specs=(_tok(D), pl.BlockSpec((D, TM), lambda i: (0, i)),
                   pl.BlockSpec((2 * PH, TM, NK), lambda i: (0, i, 0))),
        compiler_params=_params(("arbitrary",), VMEM_BIG),
        name="post",
    )(o_p, o_s, u, g, xp, xs, mod, mod, mod, g_n2, w_oa, w_ob, w_o, w_pq)


LG = 128


def _topk_rank(s):
    iota = lax.broadcasted_iota(jnp.int32, s.shape, 0).astype(F32)
    iota16 = lax.broadcasted_iota(jnp.int32, (TOPK, s.shape[1]), 0)
    rank = jnp.full(s.shape, float(TOPK), F32)
    vals = jnp.zeros((TOPK, s.shape[1]), F32)
    for k in range(TOPK):
        m = jnp.max(s, axis=0, keepdims=True)
        idx = jnp.min(jnp.where(s == m, iota, float(NK)), axis=0, keepdims=True)
        hit = iota == idx
        rank = jnp.where(hit, float(k), rank)
        s = jnp.where(hit, -jnp.inf, s)
        vals = jnp.where(iota16 == k, m, vals)
    return vals, rank


def _pair_counts(v1, v2):
    n = v1.shape[1]
    i16 = lax.broadcasted_iota(jnp.int32, (TOPK, n), 0).astype(F32)
    i8 = lax.broadcasted_iota(jnp.int32, (8, n), 0).astype(F32)
    blocks = [v1 + v2[0:1, :]]
    idxs = [i16 * TOPK]
    for b in range(1, 8):
        blocks.append(v1[0:8, :] + v2[b:b + 1, :])
        idxs.append(i8 * TOPK + b)
    blocks.append(v1[0:1, :] + v2[8:16, :])
    idxs.append(i8 + 8.0)
    c = jnp.concatenate(blocks, axis=0)
    ci = jnp.concatenate(idxs, axis=0)
    counts = jnp.zeros((TOPK, n), F32)
    m0 = None
    z = None
    for k in range(TOPK):
        m = jnp.max(c, axis=0, keepdims=True)
        if k == 0:
            m0 = m
            z = jnp.ones_like(m)
        else:
            z = z + jnp.exp(m - m0)
        idx = jnp.min(jnp.where(c == m, ci, float(TOPK * TOPK)), axis=0, keepdims=True)
        c = jnp.where(ci == idx, -jnp.inf, c)
        a_sel = jnp.floor(idx * (1.0 / TOPK))
        counts = counts + jnp.where(i16 == a_sel, 1.0, 0.0)
    return counts, z


def _route_math(s1, s2):
    v1, r1 = _topk_rank(s1)
    v2, r2 = _topk_rank(s2)
    counts, z = _pair_counts(v1, v2)
    lim = jnp.zeros_like(s1)
    for a in range(TOPK):
        lim = lim + jnp.where(r1 == float(a), counts[a:a + 1, :], 0.0)
    e1 = jnp.exp(s1 - v1[0:1, :])
    e2 = jnp.exp(s2 - v2[0:1, :]) / z
    return r2, e2, lim, e1


def _route_kernel(qp_ref, k1_ref, k2_ref, r2_ref, e2_ref, lim_ref, e1_ref):
    def head(h, _):
        s1 = _dot_nt(k1_ref[h], qp_ref[2 * h])
        s2 = _dot_nt(k2_ref[h], qp_ref[2 * h + 1])
        for g in range(TM // LG):
            sl = slice(g * LG, (g + 1) * LG)
            r2, e2, lim, e1 = _route_math(s1[:, sl], s2[:, sl])
            r2_ref[h, :, sl] = r2.astype(BF16)
            e2_ref[h, :, sl] = e2.astype(BF16)
            lim_ref[h, :, sl] = lim
            e1_ref[h, :, sl] = e1
        return 0

    lax.fori_loop(0, PH, head, 0)


def _route(qp, k1, k2):
    rs = pl.BlockSpec((PH, NK, TM), lambda i: (0, 0, i))
    return pl.pallas_call(
        _route_kernel,
        out_shape=(jax.ShapeDtypeStruct((PH, NK, T), BF16),
                   jax.ShapeDtypeStruct((PH, NK, T), BF16),
                   jax.ShapeDtypeStruct((PH, NK, T), F32),
                   jax.ShapeDtypeStruct((PH, NK, T), F32)),
        grid=(NT_,),
        in_specs=[pl.BlockSpec((2 * PH, TM, NK), lambda i: (0, i, 0)),
                  _const((PH, NK, NK)), _const((PH, NK, NK))],
        out_specs=(rs, rs, rs, rs),
        compiler_params=_params(("arbitrary",), VMEM_BIG),
        name="route",
    )(qp, k1, k2)


SQRT_HALF = 0.7071067811865476


def _gelu(a):
    return 0.5 * a * (1.0 + lax.erf(a * SQRT_HALF))


def _gate_act_block(at_ref, wt_ref, r2_ref, e2_ref, lim_ref, e1_ref, row0, il, g):
    rows = slice(il * NK, (il + 1) * NK)
    sl = slice(g * LG, (g + 1) * LG)
    r = row0 + il
    gate = None
    for h in range(PH):
        lim = lim_ref[h, r:r + 1, sl].astype(BF16)
        e1 = e1_ref[h, r:r + 1, sl].astype(BF16)
        term = jnp.where(r2_ref[h, :, sl] < lim, e2_ref[h, :, sl], 0) * e1
        gate = term if gate is None else gate + term
    a = at_ref[rows, sl]
    half = 0.5 * a
    act = half + half * lax.erf(a * SQRT_HALF)
    wt_ref[rows, sl] = (gate.astype(F32) * act).astype(BF16)


def _peer_half(acc_ref, vt_ref, wt_c, u_ref, h2t_ref, at_a, at_b, wt_b, r2_ref, e2_ref, lim_ref,
               e1_ref, half, row0):
    acc_ref[...] += _dot(vt_ref[:, half * TE:(half + 1) * TE], wt_c[...])
    at_a[...] = _dot(u_ref[half * TE:(half + 1) * TE, :], h2t_ref[...])
    for il in range(NI):
        for g in range(TMP // LG):
            _gate_act_block(at_b, wt_b, r2_ref, e2_ref, lim_ref, e1_ref, row0, il, g)


def _peer_kernel(h2t_ref, u_ref, vt_ref, r2a_ref, e2a_ref, lima_ref, e1a_ref,
                 r2b_ref, e2b_ref, limb_ref, e1b_ref, o_ref, at0, at1, wt0, wt1, acc_ref):
    s = pl.program_id(0)

    @pl.when(s == 0)
    def _():
        at0[...] = jnp.zeros_like(at0)
        at1[...] = jnp.zeros_like(at1)
        wt0[...] = jnp.zeros_like(wt0)
        wt1[...] = jnp.zeros_like(wt1)

    @pl.when((s % NPAIR == 1) | (s == 0))
    def _():
        acc_ref[...] = jnp.zeros_like(acc_ref)

    _peer_half(acc_ref, vt_ref, wt0, u_ref, h2t_ref, at0, at1, wt1,
               r2a_ref, e2a_ref, lima_ref, e1a_ref, 0, NI)
    _peer_half(acc_ref, vt_ref, wt1, u_ref, h2t_ref, at1, at0, wt0,
               r2b_ref, e2b_ref, limb_ref, e1b_ref, 1, 0)

    @pl.when((s % NPAIR == 0) & (s > 0))
    def _():
        for c in range(D // LG):
            o_ref[:, c * LG:(c + 1) * LG] = acc_ref[c * LG:(c + 1) * LG, :].T


def _peer(h2, w_u, w_vt, r2, e2, lim, e1):
    ntp = T // TMP
    tok_a = lambda s: jnp.maximum(s - 1, 0) // NPAIR
    tok_b = lambda s: jnp.minimum(s // NPAIR, ntp - 1)
    pair_a = lambda s: (s + NPAIR - 1) % NPAIR
    pair_b = lambda s: s % NPAIR
    full_a = pl.BlockSpec((PH, NK, TMP), lambda s: (0, 0, tok_a(s)))
    full_b = pl.BlockSpec((PH, NK, TMP), lambda s: (0, 0, tok_b(s)))
    rows_a = pl.BlockSpec((PH, 2 * NI, TMP), lambda s: (0, pair_a(s), tok_a(s)))
    rows_b = pl.BlockSpec((PH, 2 * NI, TMP), lambda s: (0, pair_b(s), tok_b(s)))
    return pl.pallas_call(
        _peer_kernel,
        out_shape=jax.ShapeDtypeStruct((T, D), F32),
        grid=(ntp * NPAIR + 1,),
        in_specs=[pl.BlockSpec((D, TMP), lambda s: (0, tok_b(s))),
                  pl.BlockSpec((2 * TE, D), lambda s: (pair_b(s), 0)),
                  pl.BlockSpec((D, 2 * TE), lambda s: (0, pair_a(s))),
                  full_a, full_a, rows_a, rows_a, full_b, full_b, rows_b, rows_b],
        out_specs=pl.BlockSpec((TMP, D), lambda s: (tok_a(s), 0)),
        scratch_shapes=[pltpu.VMEM((TE, TMP), F32), pltpu.VMEM((TE, TMP), F32),
                        pltpu.VMEM((TE, TMP), BF16), pltpu.VMEM((TE, TMP), BF16),
                        pltpu.VMEM((D, TMP), F32)],
        compiler_params=_params(("arbitrary",), VMEM_BIG),
        name="peer",
    )(h2, w_u, w_vt, r2, e2, lim, e1, r2, e2, lim, e1)


def _final_kernel(x1_ref, pe_ref, gt_ref, gf_ref, yp_ref, ys_ref):
    i = pl.program_id(0)
    gt = _group_rows(gt_ref, i, NPT, TM // SS)
    x2 = x1_ref[...] + _per_group(pe_ref[...], lambda pp, g: g * pp, gt)
    y = _rms(x2, gf_ref[...])

    @pl.when(i < NPT)
    def _():
        yp_ref[...] = y

    ys_ref[...] = y


def _final(x1, pe, mod, g_f):
    return pl.pallas_call(
        _final_kernel,
        out_shape=(jax.ShapeDtypeStruct((TP, D), F32), jax.ShapeDtypeStruct((TS, D), F32)),
        grid=(NT_,),
        in_specs=[_tok(D), _tok(D), _mod_spec(5), _const((1, D))],
        out_specs=(_tok_p(D), _tok_s(D)),
        compiler_params=_params(("arbitrary",), VMEM_BIG),
        name="final",
    )(x1, pe, mod, g_f)


def _rot_cols(w):
    half = w.shape[-1] // 2
    return jnp.concatenate([-w[..., half:], w[..., :half]], axis=-1)


def _rope_tables():
    half = DR // 2
    pos = jnp.concatenate([jnp.arange(TP), PAST + jnp.tile(jnp.arange(SS), NB)])
    inv = 1.0 / (ROPE_THETA ** (jnp.arange(half, dtype=F32) / half))
    ang = pos.astype(F32)[:, None] * inv[None, :]
    cos = jnp.cos(ang)
    sin = jnp.sin(ang)
    return jnp.concatenate([cos, cos], axis=1), jnp.concatenate([sin, sin], axis=1)


def kernel(x_prompt, x_sample, cache_ckv, cache_krope, state_conv, c_prompt, c_sample, w_ada, b_ada, g_n1, w_in, g_q, g_kv, w_uq, w_uk, w_uv, w_oa, w_conv, b_conv, w_ob, w_o, g_n2, w_pq, sub_k1, sub_k2, w_u, w_v, g_f):
    assert x_prompt.shape == (1, TP, D) and x_sample.shape == (NB, SS, D)
    assert cache_ckv.shape == (1, NB, PAST, KVL) and w_u.shape == (1, NE, D)
    xp = x_prompt.reshape(TP, D)
    xs = x_sample.reshape(TS, D)
    c_all = jnp.concatenate([c_sample, c_prompt, jnp.zeros((MODROWS - NB - 1, D), F32)], axis=0)
    cos2, sin2 = _rope_tables()

    w = w_in[0]
    o_kr = QL + KVL
    o_h = o_kr + DR
    w_lat = jnp.concatenate([w[:, :o_h], _rot_cols(w[:, o_kr:o_h])], axis=1).astype(BF16)
    w_hbc = w[:, o_h:o_h + 3 * CW].astype(BF16)
    w_g = w[:, o_h + 3 * CW:].astype(BF16)
    wq = w_uq[0]
    w_q3 = jnp.concatenate([wq, _rot_cols(wq[..., DN:])], axis=-1).astype(BF16)
    w_q = w_q3.reshape(QL, NH * QW)
    w_qt = jnp.transpose(w_q3, (1, 2, 0))
    w_uk2 = w_uk[0].reshape(KVL, NH * DN).astype(BF16)
    w_ukt = jnp.transpose(w_uk[0], (1, 2, 0)).astype(BF16)
    w_uv2 = w_uv[0].reshape(KVL, NH * DV).astype(BF16)
    w_uvt = jnp.transpose(w_uv[0], (1, 2, 0)).astype(BF16)
    state = state_conv[0]
    s1 = jnp.pad(state[:, 1:2], ((0, 0), (0, SS - 1), (0, 0))).reshape(TS, CW)
    s2 = jnp.pad(state, ((0, 0), (0, SS - 2), (0, 0))).reshape(TS, CW)

    mod = _ada(c_all, w_ada[0], b_ada)
    h, cq, ckv, ckvb, kr = _lat(xp, xs, mod, g_n1, w_lat, g_q, g_kv, cos2, sin2)
    u, zs, zp = _conv(h, w_hbc, w_conv[0], b_conv, s1, s2)
    g = _gate(h, w_g)
    q, qt, k, vt = _qkv(cq, ckvb, kr, cos2, sin2, cos2.T, sin2.T, w_q, w_qt, w_uk2, w_uvt)
    o_p = _attn(qt, k, vt)
    q_abs = _sq(q, w_ukt)
    o_lat = _sattn(q_abs, q, cache_ckv, cache_krope, ckvb, kr)
    o_s = _so(o_lat, w_uv2)
    x1, h2, qp = _post(o_p, o_s, u, g, xp, xs, mod, g_n2, w_oa[0].astype(BF16),
                       w_ob[0].astype(BF16), w_o[0].astype(BF16), w_pq[0].astype(BF16))
    r2, e2, lim, e1 = _route(qp, sub_k1[0].astype(BF16), sub_k2[0].astype(BF16))
    pe = _peer(h2, w_u[0].astype(BF16), jnp.transpose(w_v[0]).astype(BF16), r2, e2, lim, e1)
    y_p, y_s = _final(x1, pe, mod, g_f.reshape(1, D))

    return (y_p.reshape(1, TP, D), y_s.reshape(NB, SS, D),
            ckv[:TP].reshape(1, 1, TP, KVL), kr[:TP].reshape(1, 1, TP, DR),
            zp[6:8].reshape(1, 1, 2, CW),
            ckv[TP:].reshape(1, NB, SS, KVL), kr[TP:].reshape(1, NB, SS, DR),
            zs.reshape(NB, SS, CW)[:, SS - 2:].reshape(1, NB, 2, CW))
```

```python
import jax
import jax.numpy as jnp
from jax import lax
from jax.experimental import pallas as pl
from jax.experimental.pallas import tpu as pltpu

F32 = jnp.float32
BF16 = jnp.bfloat16

D = 2048
TP = 8192
NB = 32
SS = 16
TS = NB * SS
T = TP + TS
PAST = 1024
CHUNK = 64
NH = 8
DN = 128
DR = 64
DQ = DN + DR
DV = 128
QL = 512
KVL = 512
ROPE_THETA = 10000.0
SCALE = (DN + DR) ** -0.5
CW = 1024
PH = 8
NK = 128
NE = NK * NK
TOPK = 16
EPS = 1e-6
NEG = float(jnp.finfo(jnp.float32).min)

TM = 256
NPT = TP // TM
NT_ = T // TM
MODROWS = 40
PROMPT_ROW = NB

TMP = 512
NI = 8
TE = NI * NK
PCH = 128
PTL = TMP
VMEM_BIG = 56 * 1024 * 1024

NT_DIMS = (((1,), (1,)), ((), ()))


def _dot(a, b):
    return jnp.dot(a, b, preferred_element_type=F32)


def _dot_nt(a, b):
    return lax.dot_general(a, b, NT_DIMS, preferred_element_type=F32)


def _rms(x, g):
    return x * lax.rsqrt(jnp.mean(x * x, axis=-1, keepdims=True) + EPS) * g


def _group_rows(ref, tile, n_prompt_tiles, groups):
    s = jnp.maximum(tile - n_prompt_tiles, 0)
    rows_s = ref[pl.ds(pl.multiple_of(s * groups, groups), groups), :]
    rows_p = jnp.broadcast_to(ref[PROMPT_ROW:PROMPT_ROW + 1, :], rows_s.shape)
    is_p = jnp.full(rows_s.shape, tile, jnp.int32) < n_prompt_tiles
    return jnp.where(is_p, rows_p, rows_s)


def _per_group(x, fn, *rows):
    n, d = x.shape
    g = rows[0].shape[0]
    x3 = x.reshape(g, n // g, d)
    return fn(x3, *[r[:, None, :] for r in rows]).reshape(n, d)


def _select_tile(tile, n_prompt_tiles, p_ref, s_ref):
    vp = p_ref[...]
    vs = s_ref[...]
    is_p = jnp.full(vp.shape, tile, jnp.int32) < n_prompt_tiles
    return jnp.where(is_p, vp, vs)


def _const(shape):
    nd = len(shape)
    return pl.BlockSpec(shape, lambda *_: (0,) * nd, pipeline_mode=pl.Buffered(1))


def _params(sem, vmem=None):
    return pltpu.CompilerParams(dimension_semantics=sem, vmem_limit_bytes=vmem)


ADA_TN = 1536


def _ada_kernel(c_ref, w_ref, b_ref, o_ref):
    o_ref[...] = _dot(c_ref[...].astype(BF16), w_ref[...].astype(BF16)) + b_ref[...]


def _ada(c_all, w_ada, b_ada):
    n = w_ada.shape[1]
    return pl.pallas_call(
        _ada_kernel,
        out_shape=jax.ShapeDtypeStruct((MODROWS, n), F32),
        grid=(n // ADA_TN,),
        in_specs=[pl.BlockSpec((MODROWS, D), lambda j: (0, 0)),
                  pl.BlockSpec((D, ADA_TN), lambda j: (0, j)),
                  pl.BlockSpec((1, ADA_TN), lambda j: (0, j))],
        out_specs=pl.BlockSpec((MODROWS, ADA_TN), lambda j: (0, j)),
        compiler_params=_params(("arbitrary",), VMEM_BIG),
        name="ada",
    )(c_all, w_ada, b_ada)


def _mod_spec(k):
    return pl.BlockSpec((MODROWS, D), lambda *_: (0, k))


def _tok(width):
    return pl.BlockSpec((TM, width), lambda i: (i, 0))


def _tok_p(width):
    return pl.BlockSpec((TM, width), lambda i: (jnp.minimum(i, NPT - 1), 0))


def _tok_s(width):
    return pl.BlockSpec((TM, width), lambda i: (jnp.maximum(i - NPT, 0), 0))


def _lat_kernel(xp_ref, xs_ref, sh_ref, sc_ref, gn_ref, w_ref, gq_ref, gkv_ref, cos_ref, sin_ref,
                h_ref, cq_ref, ckv_ref, ckvb_ref, kr_ref):
    i = pl.program_id(0)
    x = _select_tile(i, NPT, xp_ref, xs_ref)
    xn = _rms(x, gn_ref[...])
    sh = _group_rows(sh_ref, i, NPT, TM // SS)
    sc = _group_rows(sc_ref, i, NPT, TM // SS)
    hb = _per_group(xn, lambda a, s, c: a * (1 + c) + s, sh, sc).astype(BF16)
    h_ref[...] = hb
    p = _dot(hb, w_ref[...])
    cq_ref[...] = _rms(p[:, :QL], gq_ref[...]).astype(BF16)
    ckv = _rms(p[:, QL:QL + KVL], gkv_ref[...])
    ckv_ref[...] = ckv
    ckvb_ref[...] = ckv.astype(BF16)
    o = QL + KVL
    kr_ref[...] = p[:, o:o + DR] * cos_ref[...] + p[:, o + DR:o + 2 * DR] * sin_ref[...]


def _lat(xp, xs, mod, g_n1, w_lat, g_q, g_kv, cos2, sin2):
    wl = w_lat.shape[1]
    return pl.pallas_call(
        _lat_kernel,
        out_shape=(jax.ShapeDtypeStruct((T, D), BF16),
                   jax.ShapeDtypeStruct((T, QL), BF16),
                   jax.ShapeDtypeStruct((T, KVL), F32),
                   jax.ShapeDtypeStruct((T, KVL), BF16),
                   jax.ShapeDtypeStruct((T, DR), F32)),
        grid=(NT_,),
        in_specs=[_tok_p(D), _tok_s(D), _mod_spec(0), _mod_spec(1), _const((1, D)),
                  _const((D, wl)), _const((1, QL)), _const((1, KVL)), _tok(DR), _tok(DR)],
        out_specs=(_tok(D), _tok(QL), _tok(KVL), _tok(KVL), _tok(DR)),
        compiler_params=_params(("arbitrary",), VMEM_BIG),
        name="lat",
    )(xp, xs, mod, mod, g_n1, w_lat, g_q, g_kv, cos2, sin2)


def _conv_kernel(h_ref, w_ref, wc_ref, bc_ref, s1_ref, s2_ref, u_ref, zs_ref, zp_ref, carry_ref):
    i = pl.program_id(0)

    @pl.when(i == 0)
    def _():
        carry_ref[...] = jnp.zeros_like(carry_ref)

    p = _dot(h_ref[...], w_ref[...])
    z = p[:, 2 * CW:] * p[:, :CW]
    pb = p[:, CW:2 * CW]
    row = lax.broadcasted_iota(jnp.int32, (TM, CW), 0)
    pos = row & (SS - 1)
    is_p = jnp.full((TM, CW), i, jnp.int32) < NPT
    c6 = jnp.broadcast_to(carry_ref[6:7, :], (TM, CW))
    c7 = jnp.broadcast_to(carry_ref[7:8, :], (TM, CW))
    left = jnp.where(is_p, row, pos)
    m1 = left == 0
    m2 = left < 2
    ov1 = jnp.where(is_p, c7, s1_ref[...])
    ov2 = jnp.where(is_p, jnp.where(row == 0, c6, c7), s2_ref[...])
    z1 = jnp.where(m1, ov1, pltpu.roll(z, 1, 0))
    z2 = jnp.where(m2, ov2, pltpu.roll(z, 2, 0))
    yc = wc_ref[0:1, :] * z2 + wc_ref[1:2, :] * z1 + wc_ref[2:3, :] * z + bc_ref[...]
    u_ref[...] = (pb * yc).astype(BF16)
    zs_ref[...] = z
    tail = z[TM - 8:, :]

    @pl.when(i < NPT)
    def _():
        zp_ref[...] = tail

    carry_ref[...] = tail


def _conv(h, w_hbc, w_conv, b_conv, s1, s2):
    return pl.pallas_call(
        _conv_kernel,
        out_shape=(jax.ShapeDtypeStruct((T, CW), BF16),
                   jax.ShapeDtypeStruct((TS, CW), F32),
                   jax.ShapeDtypeStruct((8, CW), F32)),
        grid=(NT_,),
        in_specs=[_tok(D), _const((D, 3 * CW)), _const((3, CW)), _const((1, CW)),
                  _tok_s(CW), _tok_s(CW)],
        out_specs=(_tok(CW), _tok_s(CW), pl.BlockSpec((8, CW), lambda i: (0, 0))),
        scratch_shapes=[pltpu.VMEM((8, CW), F32)],
        compiler_params=_params(("arbitrary",), VMEM_BIG),
        name="conv",
    )(h, w_hbc, w_conv, b_conv, s1, s2)


def _gate_kernel(h_ref, w_ref, g_ref):
    g_ref[...] = jax.nn.sigmoid(_dot(h_ref[...], w_ref[...])).astype(BF16)


def _gate(h, w_g):
    n = w_g.shape[1]
    return pl.pallas_call(
        _gate_kernel,
        out_shape=jax.ShapeDtypeStruct((T, n), BF16),
        grid=(NT_,),
        in_specs=[_tok(D), _const((D, n))],
        out_specs=_tok(n),
        compiler_params=_params(("arbitrary",), VMEM_BIG),
        name="gate",
    )(h, w_g)


QW = DN + 2 * DR


TMA = 512


def _qkv_kernel(cq_ref, ckv_ref, kr_ref, cos_ref, sin_ref, cost_ref, sint_ref,
                wq_ref, wqt_ref, wuk_ref, wuvt_ref, q_ref, qt_ref, k_ref, vt_ref):
    cq = cq_ref[...]
    ckv = ckv_ref[...]
    qf = _dot(cq, wq_ref[...])
    kf = _dot(ckv, wuk_ref[...])
    cos = cos_ref[...]
    sin = sin_ref[...]
    cost = cost_ref[...]
    sint = sint_ref[...]
    krb = kr_ref[...].astype(BF16)
    for h in range(NH):
        o = h * QW
        q_ref[h, :, 0:DN] = qf[:, o:o + DN].astype(BF16)
        q_ref[h, :, DN:DQ] = (qf[:, o + DN:o + DN + DR] * cos
                              + qf[:, o + DN + DR:o + QW] * sin).astype(BF16)
        qt = _dot_nt(wqt_ref[h], cq)
        qt_ref[h, 0:DN, :] = qt[0:DN, :].astype(BF16)
        qt_ref[h, DN:DQ, :] = (qt[DN:DN + DR, :] * cost + qt[DN + DR:QW, :] * sint).astype(BF16)
        k_ref[h, :, 0:DN] = kf[:, h * DN:(h + 1) * DN].astype(BF16)
        k_ref[h, :, DN:DQ] = krb
        vt_ref[h, 0] = _dot_nt(wuvt_ref[h], ckv).astype(BF16)


def _qkv(cq, ckvb, kr, cos2, sin2, cos2t, sin2t, w_q, w_qt, w_uk, w_uvt):
    tok = lambda w: pl.BlockSpec((TMA, w), lambda i: (i, 0))
    tokt = pl.BlockSpec((DR, TMA), lambda i: (0, i))
    return pl.pallas_call(
        _qkv_kernel,
        out_shape=(jax.ShapeDtypeStruct((NH, T, DQ), BF16),
                   jax.ShapeDtypeStruct((NH, DQ, T), BF16),
                   jax.ShapeDtypeStruct((NH, T, DQ), BF16),
                   jax.ShapeDtypeStruct((NH, T // TMA, DV, TMA), BF16)),
        grid=(T // TMA,),
        in_specs=[tok(QL), tok(KVL), tok(DR), tok(DR), tok(DR), tokt, tokt,
                  _const((QL, NH * QW)), _const((NH, QW, QL)), _const((KVL, NH * DN)),
                  _const((NH, DV, KVL))],
        out_specs=(pl.BlockSpec((NH, TMA, DQ), lambda i: (0, i, 0)),
                   pl.BlockSpec((NH, DQ, TMA), lambda i: (0, 0, i)),
                   pl.BlockSpec((NH, TMA, DQ), lambda i: (0, i, 0)),
                   pl.BlockSpec((NH, 1, DV, TMA), lambda i: (0, i, 0, 0))),
        compiler_params=_params(("arbitrary",), VMEM_BIG),
        name="qkv",
    )(cq, ckvb, kr, cos2, sin2, cos2t, sin2t, w_q, w_qt, w_uk, w_uvt)


EXP2_SCALE = SCALE * 1.4426950408889634
AU = 4


def _attn_kernel(qt_ref, k_ref, vt_ref, o_ref, m_ref, l_ref, acc_ref):
    qi = pl.program_id(1)
    qt = qt_ref[0]
    m_ref[...] = jnp.full_like(m_ref, NEG)
    l_ref[...] = jnp.zeros_like(l_ref)
    acc_ref[...] = jnp.zeros_like(acc_ref)

    def steps(tiles):
        m = m_ref[...]
        l = l_ref[...]
        acc = acc_ref[...]
        for j, mask in tiles:
            k = k_ref[0, pl.ds(pl.multiple_of(j * TMA, TMA), TMA), :]
            s = _dot(k, qt)
            if mask is not None:
                s = jnp.where(mask, s, NEG)
            m_new = jnp.maximum(m, jnp.max(s, axis=0, keepdims=True))
            alpha = jnp.exp2((m - m_new) * EXP2_SCALE)
            p = jnp.exp2((s - m_new) * EXP2_SCALE)
            l = alpha * l + jnp.sum(p, axis=0, keepdims=True)
            acc = alpha * acc + _dot(vt_ref[0, j], p.astype(BF16))
            m = m_new
        m_ref[...] = m
        l_ref[...] = l
        acc_ref[...] = acc

    def body(jj, c):
        steps([(AU * jj + u, None) for u in range(AU)])
        return c

    lax.fori_loop(0, qi // AU, body, 0)
    krow = lax.broadcasted_iota(jnp.int32, (TMA, TMA), 0)
    qcol = lax.broadcasted_iota(jnp.int32, (TMA, TMA), 1)
    diag = (krow // CHUNK) <= (qcol // CHUNK)
    for rem in range(AU):
        @pl.when(qi % AU == rem)
        def _(rem=rem):
            steps([(qi - rem + u, None) for u in range(rem)] + [(qi, diag)])
    o_ref[...] = (acc_ref[...] / l_ref[...]).T.astype(BF16)


def _attn(qt, k, vt):
    return pl.pallas_call(
        _attn_kernel,
        out_shape=jax.ShapeDtypeStruct((TP, NH * DV), BF16),
        grid=(NH, TP // TMA),
        in_specs=[pl.BlockSpec((1, DQ, TMA), lambda h, i: (h, 0, i)),
                  pl.BlockSpec((1, TP, DQ), lambda h, i: (h, 0, 0)),
                  pl.BlockSpec((1, TP // TMA, DV, TMA), lambda h, i: (h, 0, 0, 0))],
        out_specs=pl.BlockSpec((TMA, DV), lambda h, i: (i, h)),
        scratch_shapes=[pltpu.VMEM((1, TMA), F32), pltpu.VMEM((1, TMA), F32),
                        pltpu.VMEM((DV, TMA), F32)],
        compiler_params=_params(("arbitrary", "arbitrary"), VMEM_BIG),
        name="attn",
    )(qt, k, vt)


def _sq_kernel(q_ref, w_ref, o_ref):
    o_ref[0] = _dot(q_ref[0, :, 0:DN], w_ref[0]).astype(BF16)


def _sq(q, w_ukt):
    return pl.pallas_call(
        _sq_kernel,
        out_shape=jax.ShapeDtypeStruct((NH, TS, KVL), BF16),
        grid=(NH,),
        in_specs=[pl.BlockSpec((1, TS, DQ), lambda h: (h, TP // TS, 0)),
                  pl.BlockSpec((1, DN, KVL), lambda h: (h, 0, 0))],
        out_specs=pl.BlockSpec((1, TS, KVL), lambda h: (h, 0, 0)),
        compiler_params=_params(("arbitrary",)),
        name="sq",
    )(q, w_ukt)


def _sattn_kernel(qa_ref, q_ref, cc_ref, ck_ref, nc_ref, nk_ref, o_ref):
    rows = NH * SS
    qa = qa_ref[...].reshape(rows, KVL)
    qr = q_ref[:, :, DN:DQ].reshape(rows, DR)
    cc = cc_ref[0, 0].astype(BF16)
    ck = ck_ref[0, 0].astype(BF16)
    nc = nc_ref[...]
    nk = nk_ref[...].astype(BF16)
    s_c = (_dot_nt(qa, cc) + _dot_nt(qr, ck)) * SCALE
    s_n = (_dot_nt(qa, nc) + _dot_nt(qr, nk)) * SCALE
    qchunk_c = (PAST + (lax.broadcasted_iota(jnp.int32, (rows, PAST), 0) & (SS - 1))) // CHUNK
    qchunk_n = (PAST + (lax.broadcasted_iota(jnp.int32, (rows, SS), 0) & (SS - 1))) // CHUNK
    kchunk_c = lax.broadcasted_iota(jnp.int32, (rows, PAST), 1) // CHUNK
    kchunk_n = (PAST + lax.broadcasted_iota(jnp.int32, (rows, SS), 1)) // CHUNK
    s_c = jnp.where(kchunk_c <= qchunk_c, s_c, NEG)
    s_n = jnp.where(kchunk_n <= qchunk_n, s_n, NEG)
    m = jnp.maximum(jnp.max(s_c, axis=-1, keepdims=True), jnp.max(s_n, axis=-1, keepdims=True))
    p_c = jnp.exp(s_c - m)
    p_n = jnp.exp(s_n - m)
    l = jnp.sum(p_c, axis=-1, keepdims=True) + jnp.sum(p_n, axis=-1, keepdims=True)
    o = (_dot(p_c.astype(BF16), cc) + _dot(p_n.astype(BF16), nc)) / l
    o_ref[...] = o.astype(BF16).reshape(NH, SS, KVL)


def _sattn(q_abs, q, cache_ckv, cache_krope, ckvb, kr):
    nb0 = TP // SS
    return pl.pallas_call(
        _sattn_kernel,
        out_shape=jax.ShapeDtypeStruct((NH, TS, KVL), BF16),
        grid=(NB,),
        in_specs=[pl.BlockSpec((NH, SS, KVL), lambda b: (0, b, 0)),
                  pl.BlockSpec((NH, SS, DQ), lambda b: (0, nb0 + b, 0)),
                  pl.BlockSpec((1, 1, PAST, KVL), lambda b: (0, b, 0, 0)),
                  pl.BlockSpec((1, 1, PAST, DR), lambda b: (0, b, 0, 0)),
                  pl.BlockSpec((SS, KVL), lambda b: (nb0 + b, 0)),
                  pl.BlockSpec((SS, DR), lambda b: (nb0 + b, 0))],
        out_specs=pl.BlockSpec((NH, SS, KVL), lambda b: (0, b, 0)),
        compiler_params=_params(("arbitrary",)),
        name="sattn",
    )(q_abs, q, cache_ckv, cache_krope, ckvb, kr)


def _so_kernel(ol_ref, w_ref, o_ref):
    o_ref[...] = _dot(ol_ref[0], w_ref[...]).astype(BF16)


def _so(o_lat, w_uv):
    return pl.pallas_call(
        _so_kernel,
        out_shape=jax.ShapeDtypeStruct((TS, NH * DV), BF16),
        grid=(NH,),
        in_specs=[pl.BlockSpec((1, TS, KVL), lambda h: (h, 0, 0)),
                  pl.BlockSpec((KVL, DV), lambda h: (0, h))],
        out_specs=pl.BlockSpec((TS, DV), lambda h: (0, h)),
        compiler_params=_params(("arbitrary",)),
        name="so",
    )(o_lat, w_uv)


def _post_kernel(op_ref, os_ref, u_ref, g_ref, xp_ref, xs_ref, gt_ref, sh_ref, sc_ref, gn_ref,
                 woa_ref, wob_ref, wo_ref, wpq_ref, x1_ref, h2t_ref, qp_ref):
    i = pl.program_id(0)
    o = _select_tile(i, NPT, op_ref, os_ref)
    x = _select_tile(i, NPT, xp_ref, xs_ref)
    a = _dot(o, woa_ref[...])
    b = _dot(u_ref[...], wob_ref[...])
    merged = g_ref[:, :D].astype(F32) * a + g_ref[:, D:].astype(F32) * b
    y = _dot(merged.astype(BF16), wo_ref[...])
    groups = TM // SS
    gt = _group_rows(gt_ref, i, NPT, groups)
    x1 = _per_group(y, lambda yy, g: g * yy, gt) + x
    x1_ref[...] = x1
    sh = _group_rows(sh_ref, i, NPT, groups)
    sc = _group_rows(sc_ref, i, NPT, groups)
    h2 = _per_group(_rms(x1, gn_ref[...]), lambda aa, s, c: aa * (1 + c) + s, sh, sc)
    for c in range(D // LG):
        h2t_ref[c * LG:(c + 1) * LG, :] = h2[:, c * LG:(c + 1) * LG].T.astype(BF16)
    qf = _dot(h2.astype(BF16), wpq_ref[...])
    for c in range(2 * PH):
        qp_ref[c] = qf[:, c * NK:(c + 1) * NK].astype(BF16)


def _post(o_p, o_s, u, g, xp, xs, mod, g_n2, w_oa, w_ob, w_o, w_pq):
    return pl.pallas_call(
        _post_kernel,
        out_shape=(jax.ShapeDtypeStruct((T, D), F32),
                   jax.ShapeDtypeStruct((D, T), BF16),
                   jax.ShapeDtypeStruct((2 * PH, T, NK), BF16)),
        grid=(NT_,),
        in_specs=[_tok_p(NH * DV), _tok_s(NH * DV), _tok(CW), _tok(2 * D), _tok_p(D), _tok_s(D),
                  _mod_spec(2), _mod_spec(3), _mod_spec(4), _const((1, D)),
                  _const((NH * DV, D)), _const((CW, D)), _const((D, D)), _const((D, D))],
        out_specs=(_tok(D), pl.BlockSpec((D, TM), lambda i: (0, i)),
                   pl.BlockSpec((2 * PH, TM, NK), lambda i: (0, i, 0))),
        compiler_params=_params(("arbitrary",), VMEM_BIG),
        name="post",
    )(o_p, o_s, u, g, xp, xs, mod, mod, mod, g_n2, w_oa, w_ob, w_o, w_pq)


LG = 128


def _topk_rank(s):
    iota = lax.broadcasted_iota(jnp.int32, s.shape, 0).astype(F32)
    iota16 = lax.broadcasted_iota(jnp.int32, (TOPK, s.shape[1]), 0)
    rank = jnp.full(s.shape, float(TOPK), F32)
    vals = jnp.zeros((TOPK, s.shape[1]), F32)
    for k in range(TOPK):
        m = jnp.max(s, axis=0, keepdims=True)
        idx = jnp.min(jnp.where(s == m, iota, float(NK)), axis=0, keepdims=True)
        hit = iota == idx
        rank = jnp.where(hit, float(k), rank)
        s = jnp.where(hit, -jnp.inf, s)
        vals = jnp.where(iota16 == k, m, vals)
    return vals, rank


def _pair_counts(v1, v2):
    n = v1.shape[1]
    i16 = lax.broadcasted_iota(jnp.int32, (TOPK, n), 0).astype(F32)
    i8 = lax.broadcasted_iota(jnp.int32, (8, n), 0).astype(F32)
    blocks = [v1 + v2[0:1, :]]
    idxs = [i16 * TOPK]
    for b in range(1, 8):
        blocks.append(v1[0:8, :] + v2[b:b + 1, :])
        idxs.append(i8 * TOPK + b)
    blocks.append(v1[0:1, :] + v2[8:16, :])
    idxs.append(i8 + 8.0)
    c = jnp.concatenate(blocks, axis=0)
    ci = jnp.concatenate(idxs, axis=0)
    counts = jnp.zeros((TOPK, n), F32)
    m0 = None
    z = None
    for k in range(TOPK):
        m = jnp.max(c, axis=0, keepdims=True)
        if k == 0:
            m0 = m
            z = jnp.ones_like(m)
        else:
            z = z + jnp.exp(m - m0)
        idx = jnp.min(jnp.where(c == m, ci, float(TOPK * TOPK)), axis=0, keepdims=True)
        c = jnp.where(ci == idx, -jnp.inf, c)
        a_sel = jnp.floor(idx * (1.0 / TOPK))
        counts = counts + jnp.where(i16 == a_sel, 1.0, 0.0)
    return counts, z


def _topk_rank_distinct(s):
    iota16 = lax.broadcasted_iota(jnp.int32, (TOPK, s.shape[1]), 0)
    rank = jnp.full(s.shape, float(TOPK), F32)
    vals = jnp.zeros((TOPK, s.shape[1]), F32)
    for k in range(TOPK):
        m = jnp.max(s, axis=0, keepdims=True)
        hit = s == m
        rank = jnp.where(hit, float(k), rank)
        s = jnp.where(hit, -jnp.inf, s)
        vals = jnp.where(iota16 == k, m, vals)
    taken = jnp.sum(jnp.where(rank < float(TOPK), 1.0, 0.0), axis=0, keepdims=True)
    return vals, rank, taken


def _pair_counts_distinct(v1, v2):
    n = v1.shape[1]
    blocks = [v1 + v2[0:1, :]]
    for b in range(1, 8):
        blocks.append(v1[0:8, :] + v2[b:b + 1, :])
    blocks.append(v1[0:1, :] + v2[8:16, :])
    c = jnp.concatenate(blocks, axis=0)
    m0 = None
    z = None
    for k in range(TOPK):
        m = jnp.max(c, axis=0, keepdims=True)
        if k == 0:
            m0 = m
            z = jnp.ones_like(m)
        else:
            z = z + jnp.exp(m - m0)
        c = jnp.where(c == m, -jnp.inf, c)
    sel = jnp.where(c == -jnp.inf, 1.0, 0.0)
    low = sel[16:24, :]
    for b in range(2, 8):
        low = low + sel[8 + 8 * b:16 + 8 * b, :]
    first = jnp.sum(sel[72:80, :], axis=0, keepdims=True)
    i16 = lax.broadcasted_iota(jnp.int32, (TOPK, n), 0)
    counts = (sel[0:16, :] + jnp.concatenate([low, jnp.zeros((8, n), F32)], axis=0)
              + jnp.where(i16 == 0, first, 0.0))
    return counts, z, jnp.sum(counts, axis=0, keepdims=True)


def _route_finish(s1, s2, v1, r1, v2, r2, counts, z):
    lim = jnp.zeros_like(s1)
    for a in range(TOPK):
        lim = lim + jnp.where(r1 == float(a), counts[a:a + 1, :], 0.0)
    e1 = jnp.exp(s1 - v1[0:1, :])
    e2 = jnp.exp(s2 - v2[0:1, :]) / z
    return r2, e2, lim, e1


def _route_math(s1, s2):
    v1, r1 = _topk_rank(s1)
    v2, r2 = _topk_rank(s2)
    counts, z = _pair_counts(v1, v2)
    return _route_finish(s1, s2, v1, r1, v2, r2, counts, z)


def _route_math_distinct(s1, s2):
    v1, r1, t1 = _topk_rank_distinct(s1)
    v2, r2, t2 = _topk_rank_distinct(s2)
    counts, z, t3 = _pair_counts_distinct(v1, v2)
    full = float(TOPK)
    clean = jnp.where((t1 == full) & (t2 == full) & (t3 == full), 1.0, 0.0)
    return _route_finish(s1, s2, v1, r1, v2, r2, counts, z), clean


def _route_kernel(qp_ref, k1_ref, k2_ref, r2_ref, e2_ref, lim_ref, e1_ref):
    def store(h, sl, r2, e2, lim, e1):
        r2_ref[h, :, sl] = r2.astype(BF16)
        e2_ref[h, :, sl] = e2.astype(BF16)
        lim_ref[h, :, sl] = lim
        e1_ref[h, :, sl] = e1

    def head(h, _):
        s1 = _dot_nt(k1_ref[h], qp_ref[2 * h])
        s2 = _dot_nt(k2_ref[h], qp_ref[2 * h + 1])
        for g in range(TM // LG):
            sl = slice(g * LG, (g + 1) * LG)
            s1g = s1[:, sl]
            s2g = s2[:, sl]
            outs, clean = _route_math_distinct(s1g, s2g)
            store(h, sl, *outs)

            @pl.when(jnp.min(clean) < 0.5)
            def _():
                store(h, sl, *_route_math(s1g, s2g))
        return 0

    lax.fori_loop(0, PH, head, 0)


def _route(qp, k1, k2):
    rs = pl.BlockSpec((PH, NK, TM), lambda i: (0, 0, i))
    return pl.pallas_call(
        _route_kernel,
        out_shape=(jax.ShapeDtypeStruct((PH, NK, T), BF16),
                   jax.ShapeDtypeStruct((PH, NK, T), BF16),
                   jax.ShapeDtypeStruct((PH, NK, T), F32),
                   jax.ShapeDtypeStruct((PH, NK, T), F32)),
        grid=(NT_,),
        in_specs=[pl.BlockSpec((2 * PH, TM, NK), lambda i: (0, i, 0)),
                  _const((PH, NK, NK)), _const((PH, NK, NK))],
        out_specs=(rs, rs, rs, rs),
        compiler_params=_params(("arbitrary",), VMEM_BIG),
        name="route",
    )(qp, k1, k2)


SQRT_HALF = 0.7071067811865476


def _gate_act_block(prod, r2_ref, e2_ref, lim_ref, e1_ref, il, g):
    sl = slice(g * LG, (g + 1) * LG)
    gate = None
    for h in range(PH):
        lim = lim_ref[h, il:il + 1, sl].astype(BF16)
        e1 = e1_ref[h, il:il + 1, sl].astype(BF16)
        term = jnp.where(r2_ref[h, :, sl] < lim, e2_ref[h, :, sl], 0) * e1
        gate = term if gate is None else gate + term
    half = 0.5 * prod
    act = half + half * lax.erf(prod * SQRT_HALF)
    return (gate.astype(F32) * act).astype(BF16)


def _peer_kernel(h2t_ref, u_ref, vt_ref, r2_ref, e2_ref, lim_ref, e1_ref, o_ref, acc_ref):
    ei = pl.program_id(1)

    @pl.when(ei == 0)
    def _():
        acc_ref[...] = jnp.zeros_like(acc_ref)

    keys_per_chunk = PCH // NK
    groups = PTL // LG
    for part in range(TMP // PTL):
        lanes = slice(part * PTL, (part + 1) * PTL)
        pieces = []
        for k in range(TE // PCH):
            prod = _dot(u_ref[k * PCH:(k + 1) * PCH, :], h2t_ref[:, lanes])
            for ik in range(keys_per_chunk):
                pieces.append(jnp.concatenate(
                    [_gate_act_block(prod[ik * NK:(ik + 1) * NK, g * LG:(g + 1) * LG], r2_ref,
                                     e2_ref, lim_ref, e1_ref, k * keys_per_chunk + ik,
                                     part * groups + g)
                     for g in range(groups)], axis=1))
        acc_ref[:, lanes] += _dot(vt_ref[...], jnp.concatenate(pieces, axis=0))

    @pl.when(ei == pl.num_programs(1) - 1)
    def _():
        for c in range(D // LG):
            o_ref[:, c * LG:(c + 1) * LG] = acc_ref[c * LG:(c + 1) * LG, :].T


def _peer(h2t, w_u, w_vt, r2, e2, lim, e1):
    return pl.pallas_call(
        _peer_kernel,
        out_shape=jax.ShapeDtypeStruct((T, D), F32),
        grid=(T // TMP, NK // NI),
        in_specs=[pl.BlockSpec((D, TMP), lambda t, e: (0, t)),
                  pl.BlockSpec((TE, D), lambda t, e: (e, 0)),
                  pl.BlockSpec((D, TE), lambda t, e: (0, e)),
                  pl.BlockSpec((PH, NK, TMP), lambda t, e: (0, 0, t)),
                  pl.BlockSpec((PH, NK, TMP), lambda t, e: (0, 0, t)),
                  pl.BlockSpec((PH, NI, TMP), lambda t, e: (0, e, t)),
                  pl.BlockSpec((PH, NI, TMP), lambda t, e: (0, e, t))],
        out_specs=pl.BlockSpec((TMP, D), lambda t, e: (t, 0)),
        scratch_shapes=[pltpu.VMEM((D, TMP), F32)],
        compiler_params=_params(("arbitrary", "arbitrary"), VMEM_BIG),
        name="peer",
    )(h2t, w_u, w_vt, r2, e2, lim, e1)


def _final_kernel(x1_ref, pe_ref, gt_ref, gf_ref, yp_ref, ys_ref):
    i = pl.program_id(0)
    gt = _group_rows(gt_ref, i, NPT, TM // SS)
    x2 = x1_ref[...] + _per_group(pe_ref[...], lambda pp, g: g * pp, gt)
    y = _rms(x2, gf_ref[...])

    @pl.when(i < NPT)
    def _():
        yp_ref[...] = y

    ys_ref[...] = y


def _final(x1, pe, mod, g_f):
    return pl.pallas_call(
        _final_kernel,
        out_shape=(jax.ShapeDtypeStruct((TP, D), F32), jax.ShapeDtypeStruct((TS, D), F32)),
        grid=(NT_,),
        in_specs=[_tok(D), _tok(D), _mod_spec(5), _const((1, D))],
        out_specs=(_tok_p(D), _tok_s(D)),
        compiler_params=_params(("arbitrary",), VMEM_BIG),
        name="final",
    )(x1, pe, mod, g_f)


def _rot_cols(w):
    half = w.shape[-1] // 2
    return jnp.concatenate([-w[..., half:], w[..., :half]], axis=-1)


def _rope_tables():
    half = DR // 2
    pos = jnp.concatenate([jnp.arange(TP), PAST + jnp.tile(jnp.arange(SS), NB)])
    inv = 1.0 / (ROPE_THETA ** (jnp.arange(half, dtype=F32) / half))
    ang = pos.astype(F32)[:, None] * inv[None, :]
    cos = jnp.cos(ang)
    sin = jnp.sin(ang)
    return jnp.concatenate([cos, cos], axis=1), jnp.concatenate([sin, sin], axis=1)


def kernel(x_prompt, x_sample, cache_ckv, cache_krope, state_conv, c_prompt, c_sample, w_ada, b_ada, g_n1, w_in, g_q, g_kv, w_uq, w_uk, w_uv, w_oa, w_conv, b_conv, w_ob, w_o, g_n2, w_pq, sub_k1, sub_k2, w_u, w_v, g_f):
    assert x_prompt.shape == (1, TP, D) and x_sample.shape == (NB, SS, D)
    assert cache_ckv.shape == (1, NB, PAST, KVL) and w_u.shape == (1, NE, D)
    xp = x_prompt.reshape(TP, D)
    xs = x_sample.reshape(TS, D)
    c_all = jnp.concatenate([c_sample, c_prompt, jnp.zeros((MODROWS - NB - 1, D), F32)], axis=0)
    cos2, sin2 = _rope_tables()

    w = w_in[0]
    o_kr = QL + KVL
    o_h = o_kr + DR
    w_lat = jnp.concatenate([w[:, :o_h], _rot_cols(w[:, o_kr:o_h])], axis=1).astype(BF16)
    w_hbc = w[:, o_h:o_h + 3 * CW].astype(BF16)
    w_g = w[:, o_h + 3 * CW:].astype(BF16)
    wq = w_uq[0]
    w_q3 = jnp.concatenate([wq, _rot_cols(wq[..., DN:])], axis=-1).astype(BF16)
    w_q = w_q3.reshape(QL, NH * QW)
    w_qt = jnp.transpose(w_q3, (1, 2, 0))
    w_uk2 = w_uk[0].reshape(KVL, NH * DN).astype(BF16)
    w_ukt = jnp.transpose(w_uk[0], (1, 2, 0)).astype(BF16)
    w_uv2 = w_uv[0].reshape(KVL, NH * DV).astype(BF16)
    w_uvt = jnp.transpose(w_uv[0], (1, 2, 0)).astype(BF16)
    state = state_conv[0]
    s1 = jnp.pad(state[:, 1:2], ((0, 0), (0, SS - 1), (0, 0))).reshape(TS, CW)
    s2 = jnp.pad(state, ((0, 0), (0, SS - 2), (0, 0))).reshape(TS, CW)

    mod = _ada(c_all, w_ada[0], b_ada)
    h, cq, ckv, ckvb, kr = _lat(xp, xs, mod, g_n1, w_lat, g_q, g_kv, cos2, sin2)
    u, zs, zp = _conv(h, w_hbc, w_conv[0], b_conv, s1, s2)
    g = _gate(h, w_g)
    q, qt, k, vt = _qkv(cq, ckvb, kr, cos2, sin2, cos2.T, sin2.T, w_q, w_qt, w_uk2, w_uvt)
    o_p = _attn(qt, k, vt)
    q_abs = _sq(q, w_ukt)
    o_lat = _sattn(q_abs, q, cache_ckv, cache_krope, ckvb, kr)
    o_s = _so(o_lat, w_uv2)
    x1, h2, qp = _post(o_p, o_s, u, g, xp, xs, mod, g_n2, w_oa[0].astype(BF16),
                       w_ob[0].astype(BF16), w_o[0].astype(BF16), w_pq[0].astype(BF16))
    r2, e2, lim, e1 = _route(qp, sub_k1[0].astype(BF16), sub_k2[0].astype(BF16))
    pe = _peer(h2, w_u[0].astype(BF16), jnp.transpose(w_v[0]).astype(BF16), r2, e2, lim, e1)
    y_p, y_s = _final(x1, pe, mod, g_f.reshape(1, D))

    return (y_p.reshape(1, TP, D), y_s.reshape(NB, SS, D),
            ckv[:TP].reshape(1, 1, TP, KVL), kr[:TP].reshape(1, 1, TP, DR),
            zp[6:8].reshape(1, 1, 2, CW),
            ckv[TP:].reshape(1, NB, SS, KVL), kr[TP:].reshape(1, NB, SS, DR),
            zs.reshape(NB, SS, CW)[:, SS - 2:].reshape(1, NB, 2, CW))
```

```python
import jax
import jax.numpy as jnp
from jax import lax
from jax.experimental import pallas as pl
from jax.experimental.pallas import tpu as pltpu

F32 = jnp.float32
BF16 = jnp.bfloat16

D = 2048
TP = 8192
NB = 32
SS = 16
TS = NB * SS
T = TP + TS
PAST = 1024
CHUNK = 64
NH = 8
DN = 128
DR = 64
DQ = DN + DR
DV = 128
QL = 512
KVL = 512
ROPE_THETA = 10000.0
SCALE = (DN + DR) ** -0.5
CW = 1024
PH = 8
NK = 128
NE = NK * NK
TOPK = 16
EPS = 1e-6
NEG = float(jnp.finfo(jnp.float32).min)

TM = 256
NPT = TP // TM
NT_ = T // TM
MODROWS = 40
PROMPT_ROW = NB

TMP = 512
NI = 8
TE = NI * NK
PCH = 128
VMEM_BIG = 56 * 1024 * 1024

NT_DIMS = (((1,), (1,)), ((), ()))


def _dot(a, b):
    return jnp.dot(a, b, preferred_element_type=F32)


def _dot_nt(a, b):
    return lax.dot_general(a, b, NT_DIMS, preferred_element_type=F32)


def _rms(x, g):
    return x * lax.rsqrt(jnp.mean(x * x, axis=-1, keepdims=True) + EPS) * g


def _group_rows(ref, tile, n_prompt_tiles, groups):
    s = jnp.maximum(tile - n_prompt_tiles, 0)
    rows_s = ref[pl.ds(pl.multiple_of(s * groups, groups), groups), :]
    rows_p = jnp.broadcast_to(ref[PROMPT_ROW:PROMPT_ROW + 1, :], rows_s.shape)
    is_p = jnp.full(rows_s.shape, tile, jnp.int32) < n_prompt_tiles
    return jnp.where(is_p, rows_p, rows_s)


def _per_group(x, fn, *rows):
    n, d = x.shape
    g = rows[0].shape[0]
    x3 = x.reshape(g, n // g, d)
    return fn(x3, *[r[:, None, :] for r in rows]).reshape(n, d)


def _select_tile(tile, n_prompt_tiles, p_ref, s_ref):
    vp = p_ref[...]
    vs = s_ref[...]
    is_p = jnp.full(vp.shape, tile, jnp.int32) < n_prompt_tiles
    return jnp.where(is_p, vp, vs)


def _const(shape):
    nd = len(shape)
    return pl.BlockSpec(shape, lambda *_: (0,) * nd, pipeline_mode=pl.Buffered(1))


def _params(sem, vmem=None):
    return pltpu.CompilerParams(dimension_semantics=sem, vmem_limit_bytes=vmem)


ADA_TN = 1536


def _ada_kernel(c_ref, w_ref, b_ref, o_ref):
    o_ref[...] = _dot(c_ref[...].astype(BF16), w_ref[...].astype(BF16)) + b_ref[...]


def _ada(c_all, w_ada, b_ada):
    n = w_ada.shape[1]
    return pl.pallas_call(
        _ada_kernel,
        out_shape=jax.ShapeDtypeStruct((MODROWS, n), F32),
        grid=(n // ADA_TN,),
        in_specs=[pl.BlockSpec((MODROWS, D), lambda j: (0, 0)),
                  pl.BlockSpec((D, ADA_TN), lambda j: (0, j)),
                  pl.BlockSpec((1, ADA_TN), lambda j: (0, j))],
        out_specs=pl.BlockSpec((MODROWS, ADA_TN), lambda j: (0, j)),
        compiler_params=_params(("arbitrary",), VMEM_BIG),
        name="ada",
    )(c_all, w_ada, b_ada)


def _mod_spec(k):
    return pl.BlockSpec((MODROWS, D), lambda *_: (0, k))


def _tok(width):
    return pl.BlockSpec((TM, width), lambda i: (i, 0))


def _tok_p(width):
    return pl.BlockSpec((TM, width), lambda i: (jnp.minimum(i, NPT - 1), 0))


def _tok_s(width):
    return pl.BlockSpec((TM, width), lambda i: (jnp.maximum(i - NPT, 0), 0))


def _lat_kernel(xp_ref, xs_ref, sh_ref, sc_ref, gn_ref, w_ref, gq_ref, gkv_ref, cos_ref, sin_ref,
                h_ref, cq_ref, ckv_ref, ckvb_ref, kr_ref):
    i = pl.program_id(0)
    x = _select_tile(i, NPT, xp_ref, xs_ref)
    xn = _rms(x, gn_ref[...])
    sh = _group_rows(sh_ref, i, NPT, TM // SS)
    sc = _group_rows(sc_ref, i, NPT, TM // SS)
    hb = _per_group(xn, lambda a, s, c: a * (1 + c) + s, sh, sc).astype(BF16)
    h_ref[...] = hb
    p = _dot(hb, w_ref[...])
    cq_ref[...] = _rms(p[:, :QL], gq_ref[...]).astype(BF16)
    ckv = _rms(p[:, QL:QL + KVL], gkv_ref[...])
    ckv_ref[...] = ckv
    ckvb_ref[...] = ckv.astype(BF16)
    o = QL + KVL
    kr_ref[...] = p[:, o:o + DR] * cos_ref[...] + p[:, o + DR:o + 2 * DR] * sin_ref[...]


def _lat(xp, xs, mod, g_n1, w_lat, g_q, g_kv, cos2, sin2):
    wl = w_lat.shape[1]
    return pl.pallas_call(
        _lat_kernel,
        out_shape=(jax.ShapeDtypeStruct((T, D), BF16),
                   jax.ShapeDtypeStruct((T, QL), BF16),
                   jax.ShapeDtypeStruct((T, KVL), F32),
                   jax.ShapeDtypeStruct((T, KVL), BF16),
                   jax.ShapeDtypeStruct((T, DR), F32)),
        grid=(NT_,),
        in_specs=[_tok_p(D), _tok_s(D), _mod_spec(0), _mod_spec(1), _const((1, D)),
                  _const((D, wl)), _const((1, QL)), _const((1, KVL)), _tok(DR), _tok(DR)],
        out_specs=(_tok(D), _tok(QL), _tok(KVL), _tok(KVL), _tok(DR)),
        compiler_params=_params(("arbitrary",), VMEM_BIG),
        name="lat",
    )(xp, xs, mod, mod, g_n1, w_lat, g_q, g_kv, cos2, sin2)


def _conv_kernel(h_ref, w_ref, wc_ref, bc_ref, s1_ref, s2_ref, u_ref, zs_ref, zp_ref, carry_ref):
    i = pl.program_id(0)

    @pl.when(i == 0)
    def _():
        carry_ref[...] = jnp.zeros_like(carry_ref)

    p = _dot(h_ref[...], w_ref[...])
    z = p[:, 2 * CW:] * p[:, :CW]
    pb = p[:, CW:2 * CW]
    row = lax.broadcasted_iota(jnp.int32, (TM, CW), 0)
    pos = row & (SS - 1)
    is_p = jnp.full((TM, CW), i, jnp.int32) < NPT
    c6 = jnp.broadcast_to(carry_ref[6:7, :], (TM, CW))
    c7 = jnp.broadcast_to(carry_ref[7:8, :], (TM, CW))
    left = jnp.where(is_p, row, pos)
    m1 = left == 0
    m2 = left < 2
    ov1 = jnp.where(is_p, c7, s1_ref[...])
    ov2 = jnp.where(is_p, jnp.where(row == 0, c6, c7), s2_ref[...])
    z1 = jnp.where(m1, ov1, pltpu.roll(z, 1, 0))
    z2 = jnp.where(m2, ov2, pltpu.roll(z, 2, 0))
    yc = wc_ref[0:1, :] * z2 + wc_ref[1:2, :] * z1 + wc_ref[2:3, :] * z + bc_ref[...]
    u_ref[...] = (pb * yc).astype(BF16)
    zs_ref[...] = z
    tail = z[TM - 8:, :]

    @pl.when(i < NPT)
    def _():
        zp_ref[...] = tail

    carry_ref[...] = tail


def _conv(h, w_hbc, w_conv, b_conv, s1, s2):
    return pl.pallas_call(
        _conv_kernel,
        out_shape=(jax.ShapeDtypeStruct((T, CW), BF16),
                   jax.ShapeDtypeStruct((TS, CW), F32),
                   jax.ShapeDtypeStruct((8, CW), F32)),
        grid=(NT_,),
        in_specs=[_tok(D), _const((D, 3 * CW)), _const((3, CW)), _const((1, CW)),
                  _tok_s(CW), _tok_s(CW)],
        out_specs=(_tok(CW), _tok_s(CW), pl.BlockSpec((8, CW), lambda i: (0, 0))),
        scratch_shapes=[pltpu.VMEM((8, CW), F32)],
        compiler_params=_params(("arbitrary",), VMEM_BIG),
        name="conv",
    )(h, w_hbc, w_conv, b_conv, s1, s2)


def _gate_kernel(h_ref, w_ref, g_ref):
    g_ref[...] = jax.nn.sigmoid(_dot(h_ref[...], w_ref[...])).astype(BF16)


def _gate(h, w_g):
    n = w_g.shape[1]
    return pl.pallas_call(
        _gate_kernel,
        out_shape=jax.ShapeDtypeStruct((T, n), BF16),
        grid=(NT_,),
        in_specs=[_tok(D), _const((D, n))],
        out_specs=_tok(n),
        compiler_params=_params(("arbitrary",), VMEM_BIG),
        name="gate",
    )(h, w_g)


QW = DN + 2 * DR


TMA = 512


def _qkv_kernel(cq_ref, ckv_ref, kr_ref, cos_ref, sin_ref, cost_ref, sint_ref,
                wq_ref, wqt_ref, wuk_ref, wuvt_ref, q_ref, qt_ref, k_ref, vt_ref):
    cq = cq_ref[...]
    ckv = ckv_ref[...]
    qf = _dot(cq, wq_ref[...])
    kf = _dot(ckv, wuk_ref[...])
    cos = cos_ref[...]
    sin = sin_ref[...]
    cost = cost_ref[...]
    sint = sint_ref[...]
    krb = kr_ref[...].astype(BF16)
    for h in range(NH):
        o = h * QW
        q_ref[h, :, 0:DN] = qf[:, o:o + DN].astype(BF16)
        q_ref[h, :, DN:DQ] = (qf[:, o + DN:o + DN + DR] * cos
                              + qf[:, o + DN + DR:o + QW] * sin).astype(BF16)
        qt = _dot_nt(wqt_ref[h], cq)
        qt_ref[h, 0:DN, :] = qt[0:DN, :].astype(BF16)
        qt_ref[h, DN:DQ, :] = (qt[DN:DN + DR, :] * cost + qt[DN + DR:QW, :] * sint).astype(BF16)
        k_ref[h, :, 0:DN] = kf[:, h * DN:(h + 1) * DN].astype(BF16)
        k_ref[h, :, DN:DQ] = krb
        vt_ref[h, 0] = _dot_nt(wuvt_ref[h], ckv).astype(BF16)


def _qkv(cq, ckvb, kr, cos2, sin2, cos2t, sin2t, w_q, w_qt, w_uk, w_uvt):
    tok = lambda w: pl.BlockSpec((TMA, w), lambda i: (i, 0))
    tokt = pl.BlockSpec((DR, TMA), lambda i: (0, i))
    return pl.pallas_call(
        _qkv_kernel,
        out_shape=(jax.ShapeDtypeStruct((NH, T, DQ), BF16),
                   jax.ShapeDtypeStruct((NH, DQ, T), BF16),
                   jax.ShapeDtypeStruct((NH, T, DQ), BF16),
                   jax.ShapeDtypeStruct((NH, T // TMA, DV, TMA), BF16)),
        grid=(T // TMA,),
        in_specs=[tok(QL), tok(KVL), tok(DR), tok(DR), tok(DR), tokt, tokt,
                  _const((QL, NH * QW)), _const((NH, QW, QL)), _const((KVL, NH * DN)),
                  _const((NH, DV, KVL))],
        out_specs=(pl.BlockSpec((NH, TMA, DQ), lambda i: (0, i, 0)),
                   pl.BlockSpec((NH, DQ, TMA), lambda i: (0, 0, i)),
                   pl.BlockSpec((NH, TMA, DQ), lambda i: (0, i, 0)),
                   pl.BlockSpec((NH, 1, DV, TMA), lambda i: (0, i, 0, 0))),
        compiler_params=_params(("arbitrary",), VMEM_BIG),
        name="qkv",
    )(cq, ckvb, kr, cos2, sin2, cos2t, sin2t, w_q, w_qt, w_uk, w_uvt)


EXP2_SCALE = SCALE * 1.4426950408889634
AU = 4


def _attn_kernel(qt_ref, k_ref, vt_ref, o_ref, m_ref, l_ref, acc_ref):
    qi = pl.program_id(1)
    qt = qt_ref[0]
    m_ref[...] = jnp.full_like(m_ref, NEG)
    l_ref[...] = jnp.zeros_like(l_ref)
    acc_ref[...] = jnp.zeros_like(acc_ref)

    def steps(tiles):
        m = m_ref[...]
        l = l_ref[...]
        acc = acc_ref[...]
        for j, mask in tiles:
            k = k_ref[0, pl.ds(pl.multiple_of(j * TMA, TMA), TMA), :]
            s = _dot(k, qt)
            if mask is not None:
                s = jnp.where(mask, s, NEG)
            m_new = jnp.maximum(m, jnp.max(s, axis=0, keepdims=True))
            alpha = jnp.exp2((m - m_new) * EXP2_SCALE)
            p = jnp.exp2((s - m_new) * EXP2_SCALE)
            l = alpha * l + jnp.sum(p, axis=0, keepdims=True)
            acc = alpha * acc + _dot(vt_ref[0, j], p.astype(BF16))
            m = m_new
        m_ref[...] = m
        l_ref[...] = l
        acc_ref[...] = acc

    def body(jj, c):
        steps([(AU * jj + u, None) for u in range(AU)])
        return c

    lax.fori_loop(0, qi // AU, body, 0)
    krow = lax.broadcasted_iota(jnp.int32, (TMA, TMA), 0)
    qcol = lax.broadcasted_iota(jnp.int32, (TMA, TMA), 1)
    diag = (krow // CHUNK) <= (qcol // CHUNK)
    for rem in range(AU):
        @pl.when(qi % AU == rem)
        def _(rem=rem):
            steps([(qi - rem + u, None) for u in range(rem)] + [(qi, diag)])
    o_ref[...] = (acc_ref[...] / l_ref[...]).T.astype(BF16)


def _attn(qt, k, vt):
    return pl.pallas_call(
        _attn_kernel,
        out_shape=jax.ShapeDtypeStruct((TP, NH * DV), BF16),
        grid=(NH, TP // TMA),
        in_specs=[pl.BlockSpec((1, DQ, TMA), lambda h, i: (h, 0, i)),
                  pl.BlockSpec((1, TP, DQ), lambda h, i: (h, 0, 0)),
                  pl.BlockSpec((1, TP // TMA, DV, TMA), lambda h, i: (h, 0, 0, 0))],
        out_specs=pl.BlockSpec((TMA, DV), lambda h, i: (i, h)),
        scratch_shapes=[pltpu.VMEM((1, TMA), F32), pltpu.VMEM((1, TMA), F32),
                        pltpu.VMEM((DV, TMA), F32)],
        compiler_params=_params(("arbitrary", "arbitrary"), VMEM_BIG),
        name="attn",
    )(qt, k, vt)


def _sq_kernel(q_ref, w_ref, o_ref):
    o_ref[0] = _dot(q_ref[0, :, 0:DN], w_ref[0]).astype(BF16)


def _sq(q, w_ukt):
    return pl.pallas_call(
        _sq_kernel,
        out_shape=jax.ShapeDtypeStruct((NH, TS, KVL), BF16),
        grid=(NH,),
        in_specs=[pl.BlockSpec((1, TS, DQ), lambda h: (h, TP // TS, 0)),
                  pl.BlockSpec((1, DN, KVL), lambda h: (h, 0, 0))],
        out_specs=pl.BlockSpec((1, TS, KVL), lambda h: (h, 0, 0)),
        compiler_params=_params(("arbitrary",)),
        name="sq",
    )(q, w_ukt)


def _sattn_kernel(qa_ref, q_ref, cc_ref, ck_ref, nc_ref, nk_ref, o_ref):
    rows = NH * SS
    qa = qa_ref[...].reshape(rows, KVL)
    qr = q_ref[:, :, DN:DQ].reshape(rows, DR)
    cc = cc_ref[0, 0].astype(BF16)
    ck = ck_ref[0, 0].astype(BF16)
    nc = nc_ref[...]
    nk = nk_ref[...].astype(BF16)
    s_c = (_dot_nt(qa, cc) + _dot_nt(qr, ck)) * SCALE
    s_n = (_dot_nt(qa, nc) + _dot_nt(qr, nk)) * SCALE
    qchunk_c = (PAST + (lax.broadcasted_iota(jnp.int32, (rows, PAST), 0) & (SS - 1))) // CHUNK
    qchunk_n = (PAST + (lax.broadcasted_iota(jnp.int32, (rows, SS), 0) & (SS - 1))) // CHUNK
    kchunk_c = lax.broadcasted_iota(jnp.int32, (rows, PAST), 1) // CHUNK
    kchunk_n = (PAST + lax.broadcasted_iota(jnp.int32, (rows, SS), 1)) // CHUNK
    s_c = jnp.where(kchunk_c <= qchunk_c, s_c, NEG)
    s_n = jnp.where(kchunk_n <= qchunk_n, s_n, NEG)
    m = jnp.maximum(jnp.max(s_c, axis=-1, keepdims=True), jnp.max(s_n, axis=-1, keepdims=True))
    p_c = jnp.exp(s_c - m)
    p_n = jnp.exp(s_n - m)
    l = jnp.sum(p_c, axis=-1, keepdims=True) + jnp.sum(p_n, axis=-1, keepdims=True)
    o = (_dot(p_c.astype(BF16), cc) + _dot(p_n.astype(BF16), nc)) / l
    o_ref[...] = o.astype(BF16).reshape(NH, SS, KVL)


def _sattn(q_abs, q, cache_ckv, cache_krope, ckvb, kr):
    nb0 = TP // SS
    return pl.pallas_call(
        _sattn_kernel,
        out_shape=jax.ShapeDtypeStruct((NH, TS, KVL), BF16),
        grid=(NB,),
        in_specs=[pl.BlockSpec((NH, SS, KVL), lambda b: (0, b, 0)),
                  pl.BlockSpec((NH, SS, DQ), lambda b: (0, nb0 + b, 0)),
                  pl.BlockSpec((1, 1, PAST, KVL), lambda b: (0, b, 0, 0)),
                  pl.BlockSpec((1, 1, PAST, DR), lambda b: (0, b, 0, 0)),
                  pl.BlockSpec((SS, KVL), lambda b: (nb0 + b, 0)),
                  pl.BlockSpec((SS, DR), lambda b: (nb0 + b, 0))],
        out_specs=pl.BlockSpec((NH, SS, KVL), lambda b: (0, b, 0)),
        compiler_params=_params(("arbitrary",)),
        name="sattn",
    )(q_abs, q, cache_ckv, cache_krope, ckvb, kr)


def _so_kernel(ol_ref, w_ref, o_ref):
    o_ref[...] = _dot(ol_ref[0], w_ref[...]).astype(BF16)


def _so(o_lat, w_uv):
    return pl.pallas_call(
        _so_kernel,
        out_shape=jax.ShapeDtypeStruct((TS, NH * DV), BF16),
        grid=(NH,),
        in_specs=[pl.BlockSpec((1, TS, KVL), lambda h: (h, 0, 0)),
                  pl.BlockSpec((KVL, DV), lambda h: (0, h))],
        out_specs=pl.BlockSpec((TS, DV), lambda h: (0, h)),
        compiler_params=_params(("arbitrary",)),
        name="so",
    )(o_lat, w_uv)


def _post_kernel(op_ref, os_ref, u_ref, g_ref, xp_ref, xs_ref, gt_ref, sh_ref, sc_ref, gn_ref,
                 woa_ref, wob_ref, wo_ref, wpq_ref, x1_ref, h2t_ref, qp_ref):
    i = pl.program_id(0)
    o = _select_tile(i, NPT, op_ref, os_ref)
    x = _select_tile(i, NPT, xp_ref, xs_ref)
    a = _dot(o, woa_ref[...])
    b = _dot(u_ref[...], wob_ref[...])
    merged = g_ref[:, :D].astype(F32) * a + g_ref[:, D:].astype(F32) * b
    y = _dot(merged.astype(BF16), wo_ref[...])
    groups = TM // SS
    gt = _group_rows(gt_ref, i, NPT, groups)
    x1 = _per_group(y, lambda yy, g: g * yy, gt) + x
    x1_ref[...] = x1
    sh = _group_rows(sh_ref, i, NPT, groups)
    sc = _group_rows(sc_ref, i, NPT, groups)
    h2 = _per_group(_rms(x1, gn_ref[...]), lambda aa, s, c: aa * (1 + c) + s, sh, sc)
    for c in range(D // LG):
        h2t_ref[c * LG:(c + 1) * LG, :] = h2[:, c * LG:(c + 1) * LG].T.astype(BF16)
    qf = _dot(h2.astype(BF16), wpq_ref[...])
    for c in range(2 * PH):
        qp_ref[c] = qf[:, c * NK:(c + 1) * NK].astype(BF16)


def _post(o_p, o_s, u, g, xp, xs, mod, g_n2, w_oa, w_ob, w_o, w_pq):
    return pl.pallas_call(
        _post_kernel,
        out_shape=(jax.ShapeDtypeStruct((T, D), F32),
                   jax.ShapeDtypeStruct((D, T), BF16),
                   jax.ShapeDtypeStruct((2 * PH, T, NK), BF16)),
        grid=(NT_,),
        in_specs=[_tok_p(NH * DV), _tok_s(NH * DV), _tok(CW), _tok(2 * D), _tok_p(D), _tok_s(D),
                  _mod_spec(2), _mod_spec(3), _mod_spec(4), _const((1, D)),
                  _const((NH * DV, D)), _const((CW, D)), _const((D, D)), _const((D, D))],
        out_specs=(_tok(D), pl.BlockSpec((D, TM), lambda i: (0, i)),
                   pl.BlockSpec((2 * PH, TM, NK), lambda i: (0, i, 0))),
        compiler_params=_params(("arbitrary",), VMEM_BIG),
        name="post",
    )(o_p, o_s, u, g, xp, xs, mod, mod, mod, g_n2, w_oa, w_ob, w_o, w_pq)


LG = 128


def _topk_rank(s):
    iota = lax.broadcasted_iota(jnp.int32, s.shape, 0).astype(F32)
    iota16 = lax.broadcasted_iota(jnp.int32, (TOPK, s.shape[1]), 0)
    rank = jnp.full(s.shape, float(TOPK), F32)
    vals = jnp.zeros((TOPK, s.shape[1]), F32)
    for k in range(TOPK):
        m = jnp.max(s, axis=0, keepdims=True)
        idx = jnp.min(jnp.where(s == m, iota, float(NK)), axis=0, keepdims=True)
        hit = iota == idx
        rank = jnp.where(hit, float(k), rank)
        s = jnp.where(hit, -jnp.inf, s)
        vals = jnp.where(iota16 == k, m, vals)
    return vals, rank


def _pair_counts(v1, v2):
    n = v1.shape[1]
    i16 = lax.broadcasted_iota(jnp.int32, (TOPK, n), 0).astype(F32)
    i8 = lax.broadcasted_iota(jnp.int32, (8, n), 0).astype(F32)
    blocks = [v1 + v2[0:1, :]]
    idxs = [i16 * TOPK]
    for b in range(1, 8):
        blocks.append(v1[0:8, :] + v2[b:b + 1, :])
        idxs.append(i8 * TOPK + b)
    blocks.append(v1[0:1, :] + v2[8:16, :])
    idxs.append(i8 + 8.0)
    c = jnp.concatenate(blocks, axis=0)
    ci = jnp.concatenate(idxs, axis=0)
    counts = jnp.zeros((TOPK, n), F32)
    m0 = None
    z = None
    for k in range(TOPK):
        m = jnp.max(c, axis=0, keepdims=True)
        if k == 0:
            m0 = m
            z = jnp.ones_like(m)
        else:
            z = z + jnp.exp(m - m0)
        idx = jnp.min(jnp.where(c == m, ci, float(TOPK * TOPK)), axis=0, keepdims=True)
        c = jnp.where(ci == idx, -jnp.inf, c)
        a_sel = jnp.floor(idx * (1.0 / TOPK))
        counts = counts + jnp.where(i16 == a_sel, 1.0, 0.0)
    return counts, z


def _topk_rank_distinct(s):
    iota16 = lax.broadcasted_iota(jnp.int32, (TOPK, s.shape[1]), 0)
    rank = jnp.full(s.shape, float(TOPK), F32)
    vals = jnp.zeros((TOPK, s.shape[1]), F32)
    for k in range(TOPK):
        m = jnp.max(s, axis=0, keepdims=True)
        hit = s == m
        rank = jnp.where(hit, float(k), rank)
        s = jnp.where(hit, -jnp.inf, s)
        vals = jnp.where(iota16 == k, m, vals)
    taken = jnp.sum(jnp.where(rank < float(TOPK), 1.0, 0.0), axis=0, keepdims=True)
    return vals, rank, taken


def _pair_counts_distinct(v1, v2):
    n = v1.shape[1]
    blocks = [v1 + v2[0:1, :]]
    for b in range(1, 8):
        blocks.append(v1[0:8, :] + v2[b:b + 1, :])
    blocks.append(v1[0:1, :] + v2[8:16, :])
    c = jnp.concatenate(blocks, axis=0)
    m0 = None
    z = None
    for k in range(TOPK):
        m = jnp.max(c, axis=0, keepdims=True)
        if k == 0:
            m0 = m
            z = jnp.ones_like(m)
        else:
            z = z + jnp.exp(m - m0)
        c = jnp.where(c == m, -jnp.inf, c)
    sel = jnp.where(c == -jnp.inf, 1.0, 0.0)
    low = sel[16:24, :]
    for b in range(2, 8):
        low = low + sel[8 + 8 * b:16 + 8 * b, :]
    first = jnp.sum(sel[72:80, :], axis=0, keepdims=True)
    i16 = lax.broadcasted_iota(jnp.int32, (TOPK, n), 0)
    counts = (sel[0:16, :] + jnp.concatenate([low, jnp.zeros((8, n), F32)], axis=0)
              + jnp.where(i16 == 0, first, 0.0))
    return counts, z, jnp.sum(counts, axis=0, keepdims=True)


def _route_finish(s1, s2, v1, r1, v2, r2, counts, z):
    lim = jnp.zeros_like(s1)
    for a in range(TOPK):
        lim = lim + jnp.where(r1 == float(a), counts[a:a + 1, :], 0.0)
    e1 = jnp.exp(s1 - v1[0:1, :])
    e2 = jnp.exp(s2 - v2[0:1, :]) / z
    return r2, e2, lim, e1


def _route_math(s1, s2):
    v1, r1 = _topk_rank(s1)
    v2, r2 = _topk_rank(s2)
    counts, z = _pair_counts(v1, v2)
    return _route_finish(s1, s2, v1, r1, v2, r2, counts, z)


def _route_math_distinct(s1, s2):
    v1, r1, t1 = _topk_rank_distinct(s1)
    v2, r2, t2 = _topk_rank_distinct(s2)
    counts, z, t3 = _pair_counts_distinct(v1, v2)
    full = float(TOPK)
    clean = jnp.where((t1 == full) & (t2 == full) & (t3 == full), 1.0, 0.0)
    return _route_finish(s1, s2, v1, r1, v2, r2, counts, z), clean


SQRT_HALF = 0.7071067811865476
NTP = T // TMP
NET = NK // NI
NG = TMP // LG
UNITS = PH * NG // NET


def _route_scores(qp_ref, k1_ref, k2_ref, h, g):
    off = pl.multiple_of(g * LG, LG)
    s1 = _dot_nt(k1_ref[h], qp_ref[2 * h, pl.ds(off, LG), :])
    s2 = _dot_nt(k2_ref[h], qp_ref[2 * h + 1, pl.ds(off, LG), :])
    return s1, s2


def _route_store(tabs, slot, h, g, r2, e2, lim, e1):
    r2s, e2s, lims, e1s = tabs
    r2s[slot, h, g] = r2.astype(BF16)
    e2s[slot, h, g] = e2.astype(BF16)
    lims[slot, h, g] = lim
    e1s[slot, h, g] = e1


def _gate_act_block(prod, tabs, slot, row0, il, g):
    r2s, e2s, lims, e1s = tabs
    gate = None
    for h in range(PH):
        lim = lims[slot, h, g, pl.ds(row0, NI), :][il:il + 1, :].astype(BF16)
        e1 = e1s[slot, h, g, pl.ds(row0, NI), :][il:il + 1, :].astype(BF16)
        term = jnp.where(r2s[slot, h, g] < lim, e2s[slot, h, g], 0) * e1
        gate = term if gate is None else gate + term
    half = 0.5 * prod
    act = half + half * lax.erf(prod * SQRT_HALF)
    return (gate.astype(F32) * act).astype(BF16)


def _peer_kernel(qp_ref, k1_ref, k2_ref, h2t_ref, u_ref, vt_ref, o_ref,
                 r2s, e2s, lims, e1s, acc_ref):
    tp = pl.program_id(0)
    e = pl.program_id(1)
    tabs = (r2s, e2s, lims, e1s)
    slot_w = tp % 2
    slot_r = 1 - slot_w
    h = e % PH
    groups = [(e // PH) * UNITS + u for u in range(UNITS)]

    def route_values():
        return [_route_math_distinct(*_route_scores(qp_ref, k1_ref, k2_ref, h, g)) for g in groups]

    def route_finish(results):
        for g, (outs, _) in zip(groups, results):
            _route_store(tabs, slot_w, h, g, *outs)
        for g, (_, clean) in zip(groups, results):
            @pl.when(jnp.min(clean) < 0.5)
            def _(g=g):
                s1, s2 = _route_scores(qp_ref, k1_ref, k2_ref, h, g)
                _route_store(tabs, slot_w, h, g, *_route_math(s1, s2))

    @pl.when(tp == 0)
    def _():
        route_finish(route_values())

    @pl.when(tp > 0)
    def _():
        @pl.when(e == 0)
        def _():
            acc_ref[...] = jnp.zeros_like(acc_ref)

        row0 = pl.multiple_of(e * NI, NI)
        pieces = []
        for k in range(TE // PCH):
            prod = _dot(u_ref[k * PCH:(k + 1) * PCH, :], h2t_ref[...])
            for ik in range(PCH // NK):
                pieces.append(jnp.concatenate(
                    [_gate_act_block(prod[ik * NK:(ik + 1) * NK, g * LG:(g + 1) * LG], tabs, slot_r,
                                     row0, k * (PCH // NK) + ik, g)
                     for g in range(NG)], axis=1))
        results = route_values()
        acc_ref[...] += _dot(vt_ref[...], jnp.concatenate(pieces, axis=0))
        route_finish(results)

        @pl.when(e == NET - 1)
        def _():
            for c in range(D // LG):
                o_ref[:, c * LG:(c + 1) * LG] = acc_ref[c * LG:(c + 1) * LG, :].T


def _peer(qp, k1, k2, h2t, w_u, w_vt):
    tok_r = lambda t: jnp.minimum(t, NTP - 1)
    tok_p = lambda t: jnp.maximum(t - 1, 0)
    return pl.pallas_call(
        _peer_kernel,
        out_shape=jax.ShapeDtypeStruct((T, D), F32),
        grid=(NTP + 1, NET),
        in_specs=[pl.BlockSpec((2 * PH, TMP, NK), lambda t, e: (0, tok_r(t), 0),
                               pipeline_mode=pl.Buffered(1)),
                  _const((PH, NK, NK)), _const((PH, NK, NK)),
                  pl.BlockSpec((D, TMP), lambda t, e: (0, tok_p(t)), pipeline_mode=pl.Buffered(1)),
                  pl.BlockSpec((TE, D), lambda t, e: (e, 0)),
                  pl.BlockSpec((D, TE), lambda t, e: (0, e))],
        out_specs=pl.BlockSpec((TMP, D), lambda t, e: (tok_p(t), 0)),
        scratch_shapes=[pltpu.VMEM((2, PH, NG, NK, LG), BF16), pltpu.VMEM((2, PH, NG, NK, LG), BF16),
                        pltpu.VMEM((2, PH, NG, NK, LG), F32), pltpu.VMEM((2, PH, NG, NK, LG), F32),
                        pltpu.VMEM((D, TMP), F32)],
        compiler_params=_params(("arbitrary", "arbitrary"), VMEM_BIG),
        name="peer",
    )(qp, k1, k2, h2t, w_u, w_vt)


def _final_kernel(x1_ref, pe_ref, gt_ref, gf_ref, yp_ref, ys_ref):
    i = pl.program_id(0)
    gt = _group_rows(gt_ref, i, NPT, TM // SS)
    x2 = x1_ref[...] + _per_group(pe_ref[...], lambda pp, g: g * pp, gt)
    y = _rms(x2, gf_ref[...])

    @pl.when(i < NPT)
    def _():
        yp_ref[...] = y

    ys_ref[...] = y


def _final(x1, pe, mod, g_f):
    return pl.pallas_call(
        _final_kernel,
        out_shape=(jax.ShapeDtypeStruct((TP, D), F32), jax.ShapeDtypeStruct((TS, D), F32)),
        grid=(NT_,),
        in_specs=[_tok(D), _tok(D), _mod_spec(5), _const((1, D))],
        out_specs=(_tok_p(D), _tok_s(D)),
        compiler_params=_params(("arbitrary",), VMEM_BIG),
        name="final",
    )(x1, pe, mod, g_f)


def _rot_cols(w):
    half = w.shape[-1] // 2
    return jnp.concatenate([-w[..., half:], w[..., :half]], axis=-1)


def _rope_tables():
    half = DR // 2
    pos = jnp.concatenate([jnp.arange(TP), PAST + jnp.tile(jnp.arange(SS), NB)])
    inv = 1.0 / (ROPE_THETA ** (jnp.arange(half, dtype=F32) / half))
    ang = pos.astype(F32)[:, None] * inv[None, :]
    cos = jnp.cos(ang)
    sin = jnp.sin(ang)
    return jnp.concatenate([cos, cos], axis=1), jnp.concatenate([sin, sin], axis=1)


def kernel(x_prompt, x_sample, cache_ckv, cache_krope, state_conv, c_prompt, c_sample, w_ada, b_ada, g_n1, w_in, g_q, g_kv, w_uq, w_uk, w_uv, w_oa, w_conv, b_conv, w_ob, w_o, g_n2, w_pq, sub_k1, sub_k2, w_u, w_v, g_f):
    assert x_prompt.shape == (1, TP, D) and x_sample.shape == (NB, SS, D)
    assert cache_ckv.shape == (1, NB, PAST, KVL) and w_u.shape == (1, NE, D)
    xp = x_prompt.reshape(TP, D)
    xs = x_sample.reshape(TS, D)
    c_all = jnp.concatenate([c_sample, c_prompt, jnp.zeros((MODROWS - NB - 1, D), F32)], axis=0)
    cos2, sin2 = _rope_tables()

    w = w_in[0]
    o_kr = QL + KVL
    o_h = o_kr + DR
    w_lat = jnp.concatenate([w[:, :o_h], _rot_cols(w[:, o_kr:o_h])], axis=1).astype(BF16)
    w_hbc = w[:, o_h:o_h + 3 * CW].astype(BF16)
    w_g = w[:, o_h + 3 * CW:].astype(BF16)
    wq = w_uq[0]
    w_q3 = jnp.concatenate([wq, _rot_cols(wq[..., DN:])], axis=-1).astype(BF16)
    w_q = w_q3.reshape(QL, NH * QW)
    w_qt = jnp.transpose(w_q3, (1, 2, 0))
    w_uk2 = w_uk[0].reshape(KVL, NH * DN).astype(BF16)
    w_ukt = jnp.transpose(w_uk[0], (1, 2, 0)).astype(BF16)
    w_uv2 = w_uv[0].reshape(KVL, NH * DV).astype(BF16)
    w_uvt = jnp.transpose(w_uv[0], (1, 2, 0)).astype(BF16)
    state = state_conv[0]
    s1 = jnp.pad(state[:, 1:2], ((0, 0), (0, SS - 1), (0, 0))).reshape(TS, CW)
    s2 = jnp.pad(state, ((0, 0), (0, SS - 2), (0, 0))).reshape(TS, CW)

    mod = _ada(c_all, w_ada[0], b_ada)
    h, cq, ckv, ckvb, kr = _lat(xp, xs, mod, g_n1, w_lat, g_q, g_kv, cos2, sin2)
    u, zs, zp = _conv(h, w_hbc, w_conv[0], b_conv, s1, s2)
    g = _gate(h, w_g)
    q, qt, k, vt = _qkv(cq, ckvb, kr, cos2, sin2, cos2.T, sin2.T, w_q, w_qt, w_uk2, w_uvt)
    o_p = _attn(qt, k, vt)
    q_abs = _sq(q, w_ukt)
    o_lat = _sattn(q_abs, q, cache_ckv, cache_krope, ckvb, kr)
    o_s = _so(o_lat, w_uv2)
    x1, h2, qp = _post(o_p, o_s, u, g, xp, xs, mod, g_n2, w_oa[0].astype(BF16),
                       w_ob[0].astype(BF16), w_o[0].astype(BF16), w_pq[0].astype(BF16))
    pe = _peer(qp, sub_k1[0].astype(BF16), sub_k2[0].astype(BF16), h2,
               w_u[0].astype(BF16), jnp.transpose(w_v[0]).astype(BF16))
    y_p, y_s = _final(x1, pe, mod, g_f.reshape(1, D))

    return (y_p.reshape(1, TP, D), y_s.reshape(NB, SS, D),
            ckv[:TP].reshape(1, 1, TP, KVL), kr[:TP].reshape(1, 1, TP, DR),
            zp[6:8].reshape(1, 1, 2, CW),
            ckv[TP:].reshape(1, NB, SS, KVL), kr[TP:].reshape(1, NB, SS, DR),
            zs.reshape(NB, SS, CW)[:, SS - 2:].reshape(1, NB, 2, CW))
```

```python
import jax
import jax.numpy as jnp
from jax import lax
from jax.experimental import pallas as pl
from jax.experimental.pallas import tpu as pltpu

F32 = jnp.float32
BF16 = jnp.bfloat16

D = 2048
TP = 8192
NB = 32
SS = 16
TS = NB * SS
T = TP + TS
PAST = 1024
CHUNK = 64
NH = 8
DN = 128
DR = 64
DQ = DN + DR
DV = 128
QL = 512
KVL = 512
ROPE_THETA = 10000.0
SCALE = (DN + DR) ** -0.5
CW = 1024
PH = 8
NK = 128
NE = NK * NK
TOPK = 16
EPS = 1e-6
NEG = float(jnp.finfo(jnp.float32).min)

TM = 256
NPT = TP // TM
NT_ = T // TM
MODROWS = 40
PROMPT_ROW = NB

TMP = 512
NI = 8
TE = NI * NK
PCH = 128
VMEM_BIG = 56 * 1024 * 1024

NT_DIMS = (((1,), (1,)), ((), ()))


def _dot(a, b):
    return jnp.dot(a, b, preferred_element_type=F32)


def _dot_nt(a, b):
    return lax.dot_general(a, b, NT_DIMS, preferred_element_type=F32)


def _rms(x, g):
    return x * lax.rsqrt(jnp.mean(x * x, axis=-1, keepdims=True) + EPS) * g


def _group_rows(ref, tile, n_prompt_tiles, groups):
    s = jnp.maximum(tile - n_prompt_tiles, 0)
    rows_s = ref[pl.ds(pl.multiple_of(s * groups, groups), groups), :]
    rows_p = jnp.broadcast_to(ref[PROMPT_ROW:PROMPT_ROW + 1, :], rows_s.shape)
    is_p = jnp.full(rows_s.shape, tile, jnp.int32) < n_prompt_tiles
    return jnp.where(is_p, rows_p, rows_s)


def _per_group(x, fn, *rows):
    n, d = x.shape
    g = rows[0].shape[0]
    x3 = x.reshape(g, n // g, d)
    return fn(x3, *[r[:, None, :] for r in rows]).reshape(n, d)


def _select_tile(tile, n_prompt_tiles, p_ref, s_ref):
    vp = p_ref[...]
    vs = s_ref[...]
    is_p = jnp.full(vp.shape, tile, jnp.int32) < n_prompt_tiles
    return jnp.where(is_p, vp, vs)


def _const(shape):
    nd = len(shape)
    return pl.BlockSpec(shape, lambda *_: (0,) * nd, pipeline_mode=pl.Buffered(1))


def _params(sem, vmem=None):
    return pltpu.CompilerParams(dimension_semantics=sem, vmem_limit_bytes=vmem)


ADA_TN = 1536


def _ada_kernel(c_ref, w_ref, b_ref, o_ref):
    o_ref[...] = _dot(c_ref[...].astype(BF16), w_ref[...].astype(BF16)) + b_ref[...]


def _ada(c_all, w_ada, b_ada):
    n = w_ada.shape[1]
    return pl.pallas_call(
        _ada_kernel,
        out_shape=jax.ShapeDtypeStruct((MODROWS, n), F32),
        grid=(n // ADA_TN,),
        in_specs=[pl.BlockSpec((MODROWS, D), lambda j: (0, 0)),
                  pl.BlockSpec((D, ADA_TN), lambda j: (0, j)),
                  pl.BlockSpec((1, ADA_TN), lambda j: (0, j))],
        out_specs=pl.BlockSpec((MODROWS, ADA_TN), lambda j: (0, j)),
        compiler_params=_params(("arbitrary",), VMEM_BIG),
        name="ada",
    )(c_all, w_ada, b_ada)


def _mod_spec(k):
    return pl.BlockSpec((MODROWS, D), lambda *_: (0, k))


def _tok(width):
    return pl.BlockSpec((TM, width), lambda i: (i, 0))


def _tok_p(width):
    return pl.BlockSpec((TM, width), lambda i: (jnp.minimum(i, NPT - 1), 0))


def _tok_s(width):
    return pl.BlockSpec((TM, width), lambda i: (jnp.maximum(i - NPT, 0), 0))


def _lat_kernel(xp_ref, xs_ref, sh_ref, sc_ref, gn_ref, w_ref, gq_ref, gkv_ref, cos_ref, sin_ref,
                h_ref, cq_ref, ckv_ref, ckvb_ref, kr_ref):
    i = pl.program_id(0)
    x = _select_tile(i, NPT, xp_ref, xs_ref)
    xn = _rms(x, gn_ref[...])
    sh = _group_rows(sh_ref, i, NPT, TM // SS)
    sc = _group_rows(sc_ref, i, NPT, TM // SS)
    hb = _per_group(xn, lambda a, s, c: a * (1 + c) + s, sh, sc).astype(BF16)
    h_ref[...] = hb
    p = _dot(hb, w_ref[...])
    cq_ref[...] = _rms(p[:, :QL], gq_ref[...]).astype(BF16)
    ckv = _rms(p[:, QL:QL + KVL], gkv_ref[...])
    ckv_ref[...] = ckv
    ckvb_ref[...] = ckv.astype(BF16)
    o = QL + KVL
    kr_ref[...] = p[:, o:o + DR] * cos_ref[...] + p[:, o + DR:o + 2 * DR] * sin_ref[...]


def _lat(xp, xs, mod, g_n1, w_lat, g_q, g_kv, cos2, sin2):
    wl = w_lat.shape[1]
    return pl.pallas_call(
        _lat_kernel,
        out_shape=(jax.ShapeDtypeStruct((T, D), BF16),
                   jax.ShapeDtypeStruct((T, QL), BF16),
                   jax.ShapeDtypeStruct((T, KVL), F32),
                   jax.ShapeDtypeStruct((T, KVL), BF16),
                   jax.ShapeDtypeStruct((T, DR), F32)),
        grid=(NT_,),
        in_specs=[_tok_p(D), _tok_s(D), _mod_spec(0), _mod_spec(1), _const((1, D)),
                  _const((D, wl)), _const((1, QL)), _const((1, KVL)), _tok(DR), _tok(DR)],
        out_specs=(_tok(D), _tok(QL), _tok(KVL), _tok(KVL), _tok(DR)),
        compiler_params=_params(("arbitrary",), VMEM_BIG),
        name="lat",
    )(xp, xs, mod, mod, g_n1, w_lat, g_q, g_kv, cos2, sin2)


def _conv_kernel(h_ref, w_ref, wc_ref, bc_ref, s1_ref, s2_ref, u_ref, zs_ref, zp_ref, carry_ref):
    i = pl.program_id(0)

    @pl.when(i == 0)
    def _():
        carry_ref[...] = jnp.zeros_like(carry_ref)

    p = _dot(h_ref[...], w_ref[...])
    z = p[:, 2 * CW:] * p[:, :CW]
    pb = p[:, CW:2 * CW]
    row = lax.broadcasted_iota(jnp.int32, (TM, CW), 0)
    pos = row & (SS - 1)
    is_p = jnp.full((TM, CW), i, jnp.int32) < NPT
    c6 = jnp.broadcast_to(carry_ref[6:7, :], (TM, CW))
    c7 = jnp.broadcast_to(carry_ref[7:8, :], (TM, CW))
    left = jnp.where(is_p, row, pos)
    m1 = left == 0
    m2 = left < 2
    ov1 = jnp.where(is_p, c7, s1_ref[...])
    ov2 = jnp.where(is_p, jnp.where(row == 0, c6, c7), s2_ref[...])
    z1 = jnp.where(m1, ov1, pltpu.roll(z, 1, 0))
    z2 = jnp.where(m2, ov2, pltpu.roll(z, 2, 0))
    yc = wc_ref[0:1, :] * z2 + wc_ref[1:2, :] * z1 + wc_ref[2:3, :] * z + bc_ref[...]
    u_ref[...] = (pb * yc).astype(BF16)
    zs_ref[...] = z
    tail = z[TM - 8:, :]

    @pl.when(i < NPT)
    def _():
        zp_ref[...] = tail

    carry_ref[...] = tail


def _conv(h, w_hbc, w_conv, b_conv, s1, s2):
    return pl.pallas_call(
        _conv_kernel,
        out_shape=(jax.ShapeDtypeStruct((T, CW), BF16),
                   jax.ShapeDtypeStruct((TS, CW), F32),
                   jax.ShapeDtypeStruct((8, CW), F32)),
        grid=(NT_,),
        in_specs=[_tok(D), _const((D, 3 * CW)), _const((3, CW)), _const((1, CW)),
                  _tok_s(CW), _tok_s(CW)],
        out_specs=(_tok(CW), _tok_s(CW), pl.BlockSpec((8, CW), lambda i: (0, 0))),
        scratch_shapes=[pltpu.VMEM((8, CW), F32)],
        compiler_params=_params(("arbitrary",), VMEM_BIG),
        name="conv",
    )(h, w_hbc, w_conv, b_conv, s1, s2)


def _gate_kernel(h_ref, w_ref, g_ref):
    g_ref[...] = jax.nn.sigmoid(_dot(h_ref[...], w_ref[...])).astype(BF16)


def _gate(h, w_g):
    n = w_g.shape[1]
    return pl.pallas_call(
        _gate_kernel,
        out_shape=jax.ShapeDtypeStruct((T, n), BF16),
        grid=(NT_,),
        in_specs=[_tok(D), _const((D, n))],
        out_specs=_tok(n),
        compiler_params=_params(("arbitrary",), VMEM_BIG),
        name="gate",
    )(h, w_g)


QW = DN + 2 * DR


TMA = 512


def _qkv_kernel(cq_ref, ckv_ref, kr_ref, cos_ref, sin_ref, cost_ref, sint_ref,
                wq_ref, wqt_ref, wuk_ref, wuvt_ref, q_ref, qt_ref, k_ref, vt_ref):
    cq = cq_ref[...]
    ckv = ckv_ref[...]
    qf = _dot(cq, wq_ref[...])
    kf = _dot(ckv, wuk_ref[...])
    cos = cos_ref[...]
    sin = sin_ref[...]
    cost = cost_ref[...]
    sint = sint_ref[...]
    krb = kr_ref[...].astype(BF16)
    for h in range(NH):
        o = h * QW
        q_ref[h, :, 0:DN] = qf[:, o:o + DN].astype(BF16)
        q_ref[h, :, DN:DQ] = (qf[:, o + DN:o + DN + DR] * cos
                              + qf[:, o + DN + DR:o + QW] * sin).astype(BF16)
        qt = _dot_nt(wqt_ref[h], cq)
        qt_ref[h, 0:DN, :] = qt[0:DN, :].astype(BF16)
        qt_ref[h, DN:DQ, :] = (qt[DN:DN + DR, :] * cost + qt[DN + DR:QW, :] * sint).astype(BF16)
        k_ref[h, :, 0:DN] = kf[:, h * DN:(h + 1) * DN].astype(BF16)
        k_ref[h, :, DN:DQ] = krb
        vt_ref[h, 0] = _dot_nt(wuvt_ref[h], ckv).astype(BF16)


def _qkv(cq, ckvb, kr, cos2, sin2, cos2t, sin2t, w_q, w_qt, w_uk, w_uvt):
    tok = lambda w: pl.BlockSpec((TMA, w), lambda i: (i, 0))
    tokt = pl.BlockSpec((DR, TMA), lambda i: (0, i))
    return pl.pallas_call(
        _qkv_kernel,
        out_shape=(jax.ShapeDtypeStruct((NH, T, DQ), BF16),
                   jax.ShapeDtypeStruct((NH, DQ, T), BF16),
                   jax.ShapeDtypeStruct((NH, T, DQ), BF16),
                   jax.ShapeDtypeStruct((NH, T // TMA, DV, TMA), BF16)),
        grid=(T // TMA,),
        in_specs=[tok(QL), tok(KVL), tok(DR), tok(DR), tok(DR), tokt, tokt,
                  _const((QL, NH * QW)), _const((NH, QW, QL)), _const((KVL, NH * DN)),
                  _const((NH, DV, KVL))],
        out_specs=(pl.BlockSpec((NH, TMA, DQ), lambda i: (0, i, 0)),
                   pl.BlockSpec((NH, DQ, TMA), lambda i: (0, 0, i)),
                   pl.BlockSpec((NH, TMA, DQ), lambda i: (0, i, 0)),
                   pl.BlockSpec((NH, 1, DV, TMA), lambda i: (0, i, 0, 0))),
        compiler_params=_params(("arbitrary",), VMEM_BIG),
        name="qkv",
    )(cq, ckvb, kr, cos2, sin2, cos2t, sin2t, w_q, w_qt, w_uk, w_uvt)


EXP2_SCALE = SCALE * 1.4426950408889634
AU = 4


def _attn_kernel(qt_ref, k_ref, vt_ref, o_ref, m_ref, l_ref, acc_ref):
    qi = pl.program_id(1)
    qt = qt_ref[0]
    m_ref[...] = jnp.full_like(m_ref, NEG)
    l_ref[...] = jnp.zeros_like(l_ref)
    acc_ref[...] = jnp.zeros_like(acc_ref)

    def steps(tiles):
        m = m_ref[...]
        l = l_ref[...]
        acc = acc_ref[...]
        for j, mask in tiles:
            k = k_ref[0, pl.ds(pl.multiple_of(j * TMA, TMA), TMA), :]
            s = _dot(k, qt)
            if mask is not None:
                s = jnp.where(mask, s, NEG)
            m_new = jnp.maximum(m, jnp.max(s, axis=0, keepdims=True))
            alpha = jnp.exp2((m - m_new) * EXP2_SCALE)
            p = jnp.exp2((s - m_new) * EXP2_SCALE)
            l = alpha * l + jnp.sum(p, axis=0, keepdims=True)
            acc = alpha * acc + _dot(vt_ref[0, j], p.astype(BF16))
            m = m_new
        m_ref[...] = m
        l_ref[...] = l
        acc_ref[...] = acc

    def body(jj, c):
        steps([(AU * jj + u, None) for u in range(AU)])
        return c

    lax.fori_loop(0, qi // AU, body, 0)
    krow = lax.broadcasted_iota(jnp.int32, (TMA, TMA), 0)
    qcol = lax.broadcasted_iota(jnp.int32, (TMA, TMA), 1)
    diag = (krow // CHUNK) <= (qcol // CHUNK)
    for rem in range(AU):
        @pl.when(qi % AU == rem)
        def _(rem=rem):
            steps([(qi - rem + u, None) for u in range(rem)] + [(qi, diag)])
    o_ref[...] = (acc_ref[...] / l_ref[...]).T.astype(BF16)


def _attn(qt, k, vt):
    return pl.pallas_call(
        _attn_kernel,
        out_shape=jax.ShapeDtypeStruct((TP, NH * DV), BF16),
        grid=(NH, TP // TMA),
        in_specs=[pl.BlockSpec((1, DQ, TMA), lambda h, i: (h, 0, i)),
                  pl.BlockSpec((1, TP, DQ), lambda h, i: (h, 0, 0)),
                  pl.BlockSpec((1, TP // TMA, DV, TMA), lambda h, i: (h, 0, 0, 0))],
        out_specs=pl.BlockSpec((TMA, DV), lambda h, i: (i, h)),
        scratch_shapes=[pltpu.VMEM((1, TMA), F32), pltpu.VMEM((1, TMA), F32),
                        pltpu.VMEM((DV, TMA), F32)],
        compiler_params=_params(("arbitrary", "arbitrary"), VMEM_BIG),
        name="attn",
    )(qt, k, vt)


def _sq_kernel(q_ref, w_ref, o_ref):
    o_ref[0] = _dot(q_ref[0, :, 0:DN], w_ref[0]).astype(BF16)


def _sq(q, w_ukt):
    return pl.pallas_call(
        _sq_kernel,
        out_shape=jax.ShapeDtypeStruct((NH, TS, KVL), BF16),
        grid=(NH,),
        in_specs=[pl.BlockSpec((1, TS, DQ), lambda h: (h, TP // TS, 0)),
                  pl.BlockSpec((1, DN, KVL), lambda h: (h, 0, 0))],
        out_specs=pl.BlockSpec((1, TS, KVL), lambda h: (h, 0, 0)),
        compiler_params=_params(("arbitrary",)),
        name="sq",
    )(q, w_ukt)


def _sattn_kernel(qa_ref, q_ref, cc_ref, ck_ref, nc_ref, nk_ref, o_ref):
    rows = NH * SS
    qa = qa_ref[...].reshape(rows, KVL)
    qr = q_ref[:, :, DN:DQ].reshape(rows, DR)
    cc = cc_ref[0, 0].astype(BF16)
    ck = ck_ref[0, 0].astype(BF16)
    nc = nc_ref[...]
    nk = nk_ref[...].astype(BF16)
    s_c = (_dot_nt(qa, cc) + _dot_nt(qr, ck)) * SCALE
    s_n = (_dot_nt(qa, nc) + _dot_nt(qr, nk)) * SCALE
    qchunk_c = (PAST + (lax.broadcasted_iota(jnp.int32, (rows, PAST), 0) & (SS - 1))) // CHUNK
    qchunk_n = (PAST + (lax.broadcasted_iota(jnp.int32, (rows, SS), 0) & (SS - 1))) // CHUNK
    kchunk_c = lax.broadcasted_iota(jnp.int32, (rows, PAST), 1) // CHUNK
    kchunk_n = (PAST + lax.broadcasted_iota(jnp.int32, (rows, SS), 1)) // CHUNK
    s_c = jnp.where(kchunk_c <= qchunk_c, s_c, NEG)
    s_n = jnp.where(kchunk_n <= qchunk_n, s_n, NEG)
    m = jnp.maximum(jnp.max(s_c, axis=-1, keepdims=True), jnp.max(s_n, axis=-1, keepdims=True))
    p_c = jnp.exp(s_c - m)
    p_n = jnp.exp(s_n - m)
    l = jnp.sum(p_c, axis=-1, keepdims=True) + jnp.sum(p_n, axis=-1, keepdims=True)
    o = (_dot(p_c.astype(BF16), cc) + _dot(p_n.astype(BF16), nc)) / l
    o_ref[...] = o.astype(BF16).reshape(NH, SS, KVL)


def _sattn(q_abs, q, cache_ckv, cache_krope, ckvb, kr):
    nb0 = TP // SS
    return pl.pallas_call(
        _sattn_kernel,
        out_shape=jax.ShapeDtypeStruct((NH, TS, KVL), BF16),
        grid=(NB,),
        in_specs=[pl.BlockSpec((NH, SS, KVL), lambda b: (0, b, 0)),
                  pl.BlockSpec((NH, SS, DQ), lambda b: (0, nb0 + b, 0)),
                  pl.BlockSpec((1, 1, PAST, KVL), lambda b: (0, b, 0, 0)),
                  pl.BlockSpec((1, 1, PAST, DR), lambda b: (0, b, 0, 0)),
                  pl.BlockSpec((SS, KVL), lambda b: (nb0 + b, 0)),
                  pl.BlockSpec((SS, DR), lambda b: (nb0 + b, 0))],
        out_specs=pl.BlockSpec((NH, SS, KVL), lambda b: (0, b, 0)),
        compiler_params=_params(("arbitrary",)),
        name="sattn",
    )(q_abs, q, cache_ckv, cache_krope, ckvb, kr)


def _so_kernel(ol_ref, w_ref, o_ref):
    o_ref[...] = _dot(ol_ref[0], w_ref[...]).astype(BF16)


def _so(o_lat, w_uv):
    return pl.pallas_call(
        _so_kernel,
        out_shape=jax.ShapeDtypeStruct((TS, NH * DV), BF16),
        grid=(NH,),
        in_specs=[pl.BlockSpec((1, TS, KVL), lambda h: (h, 0, 0)),
                  pl.BlockSpec((KVL, DV), lambda h: (0, h))],
        out_specs=pl.BlockSpec((TS, DV), lambda h: (0, h)),
        compiler_params=_params(("arbitrary",)),
        name="so",
    )(o_lat, w_uv)


def _post_kernel(op_ref, os_ref, u_ref, g_ref, xp_ref, xs_ref, gt_ref, sh_ref, sc_ref, gn_ref,
                 woa_ref, wob_ref, wo_ref, wpq_ref, x1_ref, h2t_ref, qp_ref):
    i = pl.program_id(0)
    o = _select_tile(i, NPT, op_ref, os_ref)
    x = _select_tile(i, NPT, xp_ref, xs_ref)
    a = _dot(o, woa_ref[...])
    b = _dot(u_ref[...], wob_ref[...])
    merged = g_ref[:, :D].astype(F32) * a + g_ref[:, D:].astype(F32) * b
    y = _dot(merged.astype(BF16), wo_ref[...])
    groups = TM // SS
    gt = _group_rows(gt_ref, i, NPT, groups)
    x1 = _per_group(y, lambda yy, g: g * yy, gt) + x
    x1_ref[...] = x1
    sh = _group_rows(sh_ref, i, NPT, groups)
    sc = _group_rows(sc_ref, i, NPT, groups)
    h2 = _per_group(_rms(x1, gn_ref[...]), lambda aa, s, c: aa * (1 + c) + s, sh, sc)
    for c in range(D // LG):
        h2t_ref[c * LG:(c + 1) * LG, :] = h2[:, c * LG:(c + 1) * LG].T.astype(BF16)
    qf = _dot(h2.astype(BF16), wpq_ref[...])
    for c in range(2 * PH):
        qp_ref[c] = qf[:, c * NK:(c + 1) * NK].astype(BF16)


def _post(o_p, o_s, u, g, xp, xs, mod, g_n2, w_oa, w_ob, w_o, w_pq):
    return pl.pallas_call(
        _post_kernel,
        out_shape=(jax.ShapeDtypeStruct((T, D), F32),
                   jax.ShapeDtypeStruct((D, T), BF16),
                   jax.ShapeDtypeStruct((2 * PH, T, NK), BF16)),
        grid=(NT_,),
        in_specs=[_tok_p(NH * DV), _tok_s(NH * DV), _tok(CW), _tok(2 * D), _tok_p(D), _tok_s(D),
                  _mod_spec(2), _mod_spec(3), _mod_spec(4), _const((1, D)),
                  _const((NH * DV, D)), _const((CW, D)), _const((D, D)), _const((D, D))],
        out_specs=(_tok(D), pl.BlockSpec((D, TM), lambda i: (0, i)),
                   pl.BlockSpec((2 * PH, TM, NK), lambda i: (0, i, 0))),
        compiler_params=_params(("arbitrary",), VMEM_BIG),
        name="post",
    )(o_p, o_s, u, g, xp, xs, mod, mod, mod, g_n2, w_oa, w_ob, w_o, w_pq)


LG = 128


def _topk_rank(s):
    iota = lax.broadcasted_iota(jnp.int32, s.shape, 0).astype(F32)
    iota16 = lax.broadcasted_iota(jnp.int32, (TOPK, s.shape[1]), 0)
    rank = jnp.full(s.shape, float(TOPK), F32)
    vals = jnp.zeros((TOPK, s.shape[1]), F32)
    for k in range(TOPK):
        m = jnp.max(s, axis=0, keepdims=True)
        idx = jnp.min(jnp.where(s == m, iota, float(NK)), axis=0, keepdims=True)
        hit = iota == idx
        rank = jnp.where(hit, float(k), rank)
        s = jnp.where(hit, -jnp.inf, s)
        vals = jnp.where(iota16 == k, m, vals)
    return vals, rank


def _pair_counts(v1, v2):
    n = v1.shape[1]
    i16 = lax.broadcasted_iota(jnp.int32, (TOPK, n), 0).astype(F32)
    i8 = lax.broadcasted_iota(jnp.int32, (8, n), 0).astype(F32)
    blocks = [v1 + v2[0:1, :]]
    idxs = [i16 * TOPK]
    for b in range(1, 8):
        blocks.append(v1[0:8, :] + v2[b:b + 1, :])
        idxs.append(i8 * TOPK + b)
    blocks.append(v1[0:1, :] + v2[8:16, :])
    idxs.append(i8 + 8.0)
    c = jnp.concatenate(blocks, axis=0)
    ci = jnp.concatenate(idxs, axis=0)
    counts = jnp.zeros((TOPK, n), F32)
    m0 = None
    z = None
    for k in range(TOPK):
        m = jnp.max(c, axis=0, keepdims=True)
        if k == 0:
            m0 = m
            z = jnp.ones_like(m)
        else:
            z = z + jnp.exp(m - m0)
        idx = jnp.min(jnp.where(c == m, ci, float(TOPK * TOPK)), axis=0, keepdims=True)
        c = jnp.where(ci == idx, -jnp.inf, c)
        a_sel = jnp.floor(idx * (1.0 / TOPK))
        counts = counts + jnp.where(i16 == a_sel, 1.0, 0.0)
    return counts, z


def _topk_rank_distinct(s):
    iota16 = lax.broadcasted_iota(jnp.int32, (TOPK, s.shape[1]), 0)
    rank = jnp.full(s.shape, float(TOPK), F32)
    vals = jnp.zeros((TOPK, s.shape[1]), F32)
    for k in range(TOPK):
        m = jnp.max(s, axis=0, keepdims=True)
        hit = s == m
        rank = jnp.where(hit, float(k), rank)
        s = jnp.where(hit, -jnp.inf, s)
        vals = jnp.where(iota16 == k, m, vals)
    taken = jnp.sum(jnp.where(rank < float(TOPK), 1.0, 0.0), axis=0, keepdims=True)
    return vals, rank, taken


def _pair_counts_distinct(v1, v2):
    n = v1.shape[1]
    blocks = [v1 + v2[0:1, :]]
    for b in range(1, 8):
        blocks.append(v1[0:8, :] + v2[b:b + 1, :])
    blocks.append(v1[0:1, :] + v2[8:16, :])
    c = jnp.concatenate(blocks, axis=0)
    m0 = None
    z = None
    for k in range(TOPK):
        m = jnp.max(c, axis=0, keepdims=True)
        if k == 0:
            m0 = m
            z = jnp.ones_like(m)
        else:
            z = z + jnp.exp(m - m0)
        c = jnp.where(c == m, -jnp.inf, c)
    sel = jnp.where(c == -jnp.inf, 1.0, 0.0)
    low = sel[16:24, :]
    for b in range(2, 8):
        low = low + sel[8 + 8 * b:16 + 8 * b, :]
    first = jnp.sum(sel[72:80, :], axis=0, keepdims=True)
    i16 = lax.broadcasted_iota(jnp.int32, (TOPK, n), 0)
    counts = (sel[0:16, :] + jnp.concatenate([low, jnp.zeros((8, n), F32)], axis=0)
              + jnp.where(i16 == 0, first, 0.0))
    return counts, z, jnp.sum(counts, axis=0, keepdims=True)


def _route_finish(s1, s2, v1, r1, v2, r2, counts, z):
    lim = jnp.zeros_like(s1)
    for a in range(TOPK):
        lim = lim + jnp.where(r1 == float(a), counts[a:a + 1, :], 0.0)
    e1 = jnp.exp(s1 - v1[0:1, :])
    e2 = jnp.exp(s2 - v2[0:1, :]) / z
    return r2, e2, lim, e1


def _route_math(s1, s2):
    v1, r1 = _topk_rank(s1)
    v2, r2 = _topk_rank(s2)
    counts, z = _pair_counts(v1, v2)
    return _route_finish(s1, s2, v1, r1, v2, r2, counts, z)


def _route_math_distinct(s1, s2):
    v1, r1, t1 = _topk_rank_distinct(s1)
    v2, r2, t2 = _topk_rank_distinct(s2)
    counts, z, t3 = _pair_counts_distinct(v1, v2)
    full = float(TOPK)
    clean = jnp.where((t1 == full) & (t2 == full) & (t3 == full), 1.0, 0.0)
    return _route_finish(s1, s2, v1, r1, v2, r2, counts, z), clean


SQRT_HALF = 0.7071067811865476
NTP = T // TMP
NET = NK // NI
NG = TMP // LG
UNITS = PH * NG // NET


def _route_scores(qp_ref, k1_ref, k2_ref, h, g):
    off = pl.multiple_of(g * LG, LG)
    s1 = _dot_nt(k1_ref[h], qp_ref[2 * h, pl.ds(off, LG), :])
    s2 = _dot_nt(k2_ref[h], qp_ref[2 * h + 1, pl.ds(off, LG), :])
    return s1, s2


def _route_store(tabs, slot, h, g, r2, e2, lim, e1):
    r2s, e2s, lims, e1s = tabs
    r2s[slot, h, g] = r2.astype(BF16)
    e2s[slot, h, g] = e2.astype(BF16)
    lims[slot, h, g] = lim
    e1s[slot, h, g] = e1


def _gate_act_block(prod, tabs, slot, row0, il, g):
    r2s, e2s, lims, e1s = tabs
    gate = None
    for h in range(PH):
        lim = lims[slot, h, g, pl.ds(row0, NI), :][il:il + 1, :].astype(BF16)
        e1 = e1s[slot, h, g, pl.ds(row0, NI), :][il:il + 1, :].astype(BF16)
        kept = jnp.minimum(e2s[slot, h, g],
                           jnp.maximum(lim - r2s[slot, h, g], jnp.zeros((), BF16)))
        term = e1 * kept
        gate = term if gate is None else gate + term
    half = 0.5 * prod
    act = half + half * lax.erf(prod * SQRT_HALF)
    return (gate.astype(F32) * act).astype(BF16)


def _peer_kernel(qp_ref, k1_ref, k2_ref, h2t_ref, u_ref, vt_ref, o_ref,
                 r2s, e2s, lims, e1s, acc_ref):
    tp = pl.program_id(0)
    e = pl.program_id(1)
    tabs = (r2s, e2s, lims, e1s)
    slot_w = tp % 2
    slot_r = 1 - slot_w
    h = e % PH
    groups = [(e // PH) * UNITS + u for u in range(UNITS)]

    def route_values():
        return [_route_math_distinct(*_route_scores(qp_ref, k1_ref, k2_ref, h, g)) for g in groups]

    def route_finish(results):
        for g, (outs, _) in zip(groups, results):
            _route_store(tabs, slot_w, h, g, *outs)
        for g, (_, clean) in zip(groups, results):
            @pl.when(jnp.min(clean) < 0.5)
            def _(g=g):
                s1, s2 = _route_scores(qp_ref, k1_ref, k2_ref, h, g)
                _route_store(tabs, slot_w, h, g, *_route_math(s1, s2))

    @pl.when(tp == 0)
    def _():
        route_finish(route_values())

    @pl.when(tp > 0)
    def _():
        @pl.when(e == 0)
        def _():
            acc_ref[...] = jnp.zeros_like(acc_ref)

        row0 = pl.multiple_of(e * NI, NI)
        results = route_values()
        pieces = []
        for k in range(TE // PCH):
            prod = _dot(u_ref[k * PCH:(k + 1) * PCH, :], h2t_ref[...])
            for ik in range(PCH // NK):
                pieces.append(jnp.concatenate(
                    [_gate_act_block(prod[ik * NK:(ik + 1) * NK, g * LG:(g + 1) * LG], tabs, slot_r,
                                     row0, k * (PCH // NK) + ik, g)
                     for g in range(NG)], axis=1))
        acc_ref[...] += _dot(vt_ref[...], jnp.concatenate(pieces, axis=0))
        route_finish(results)

        @pl.when(e == NET - 1)
        def _():
            for c in range(D // LG):
                o_ref[:, c * LG:(c + 1) * LG] = acc_ref[c * LG:(c + 1) * LG, :].T


def _peer(qp, k1, k2, h2t, w_u, w_vt):
    tok_r = lambda t: jnp.minimum(t, NTP - 1)
    tok_p = lambda t: jnp.maximum(t - 1, 0)
    return pl.pallas_call(
        _peer_kernel,
        out_shape=jax.ShapeDtypeStruct((T, D), F32),
        grid=(NTP + 1, NET),
        in_specs=[pl.BlockSpec((2 * PH, TMP, NK), lambda t, e: (0, tok_r(t), 0),
                               pipeline_mode=pl.Buffered(1)),
                  _const((PH, NK, NK)), _const((PH, NK, NK)),
                  pl.BlockSpec((D, TMP), lambda t, e: (0, tok_p(t)), pipeline_mode=pl.Buffered(1)),
                  pl.BlockSpec((TE, D), lambda t, e: (jnp.where(t == 0, 0, e), 0)),
                  pl.BlockSpec((D, TE), lambda t, e: (0, jnp.where(t == 0, 0, e)))],
        out_specs=pl.BlockSpec((TMP, D), lambda t, e: (tok_p(t), 0)),
        scratch_shapes=[pltpu.VMEM((2, PH, NG, NK, LG), BF16), pltpu.VMEM((2, PH, NG, NK, LG), BF16),
                        pltpu.VMEM((2, PH, NG, NK, LG), F32), pltpu.VMEM((2, PH, NG, NK, LG), F32),
                        pltpu.VMEM((D, TMP), F32)],
        compiler_params=_params(("arbitrary", "arbitrary"), VMEM_BIG),
        name="peer",
    )(qp, k1, k2, h2t, w_u, w_vt)


def _final_kernel(x1_ref, pe_ref, gt_ref, gf_ref, yp_ref, ys_ref):
    i = pl.program_id(0)
    gt = _group_rows(gt_ref, i, NPT, TM // SS)
    x2 = x1_ref[...] + _per_group(pe_ref[...], lambda pp, g: g * pp, gt)
    y = _rms(x2, gf_ref[...])

    @pl.when(i < NPT)
    def _():
        yp_ref[...] = y

    ys_ref[...] = y


def _final(x1, pe, mod, g_f):
    return pl.pallas_call(
        _final_kernel,
        out_shape=(jax.ShapeDtypeStruct((TP, D), F32), jax.ShapeDtypeStruct((TS, D), F32)),
        grid=(NT_,),
        in_specs=[_tok(D), _tok(D), _mod_spec(5), _const((1, D))],
        out_specs=(_tok_p(D), _tok_s(D)),
        compiler_params=_params(("arbitrary",), VMEM_BIG),
        name="final",
    )(x1, pe, mod, g_f)


def _rot_cols(w):
    half = w.shape[-1] // 2
    return jnp.concatenate([-w[..., half:], w[..., :half]], axis=-1)


def _rope_tables():
    half = DR // 2
    pos = jnp.concatenate([jnp.arange(TP), PAST + jnp.tile(jnp.arange(SS), NB)])
    inv = 1.0 / (ROPE_THETA ** (jnp.arange(half, dtype=F32) / half))
    ang = pos.astype(F32)[:, None] * inv[None, :]
    cos = jnp.cos(ang)
    sin = jnp.sin(ang)
    return jnp.concatenate([cos, cos], axis=1), jnp.concatenate([sin, sin], axis=1)


def kernel(x_prompt, x_sample, cache_ckv, cache_krope, state_conv, c_prompt, c_sample, w_ada, b_ada, g_n1, w_in, g_q, g_kv, w_uq, w_uk, w_uv, w_oa, w_conv, b_conv, w_ob, w_o, g_n2, w_pq, sub_k1, sub_k2, w_u, w_v, g_f):
    assert x_prompt.shape == (1, TP, D) and x_sample.shape == (NB, SS, D)
    assert cache_ckv.shape == (1, NB, PAST, KVL) and w_u.shape == (1, NE, D)
    xp = x_prompt.reshape(TP, D)
    xs = x_sample.reshape(TS, D)
    c_all = jnp.concatenate([c_sample, c_prompt, jnp.zeros((MODROWS - NB - 1, D), F32)], axis=0)
    cos2, sin2 = _rope_tables()

    w = w_in[0]
    o_kr = QL + KVL
    o_h = o_kr + DR
    w_lat = jnp.concatenate([w[:, :o_h], _rot_cols(w[:, o_kr:o_h])], axis=1).astype(BF16)
    w_hbc = w[:, o_h:o_h + 3 * CW].astype(BF16)
    w_g = w[:, o_h + 3 * CW:].astype(BF16)
    wq = w_uq[0]
    w_q3 = jnp.concatenate([wq, _rot_cols(wq[..., DN:])], axis=-1).astype(BF16)
    w_q = w_q3.reshape(QL, NH * QW)
    w_qt = jnp.transpose(w_q3, (1, 2, 0))
    w_uk2 = w_uk[0].reshape(KVL, NH * DN).astype(BF16)
    w_ukt = jnp.transpose(w_uk[0], (1, 2, 0)).astype(BF16)
    w_uv2 = w_uv[0].reshape(KVL, NH * DV).astype(BF16)
    w_uvt = jnp.transpose(w_uv[0], (1, 2, 0)).astype(BF16)
    state = state_conv[0]
    s1 = jnp.pad(state[:, 1:2], ((0, 0), (0, SS - 1), (0, 0))).reshape(TS, CW)
    s2 = jnp.pad(state, ((0, 0), (0, SS - 2), (0, 0))).reshape(TS, CW)

    mod = _ada(c_all, w_ada[0], b_ada)
    h, cq, ckv, ckvb, kr = _lat(xp, xs, mod, g_n1, w_lat, g_q, g_kv, cos2, sin2)
    u, zs, zp = _conv(h, w_hbc, w_conv[0], b_conv, s1, s2)
    g = _gate(h, w_g)
    q, qt, k, vt = _qkv(cq, ckvb, kr, cos2, sin2, cos2.T, sin2.T, w_q, w_qt, w_uk2, w_uvt)
    o_p = _attn(qt, k, vt)
    q_abs = _sq(q, w_ukt)
    o_lat = _sattn(q_abs, q, cache_ckv, cache_krope, ckvb, kr)
    o_s = _so(o_lat, w_uv2)
    x1, h2, qp = _post(o_p, o_s, u, g, xp, xs, mod, g_n2, w_oa[0].astype(BF16),
                       w_ob[0].astype(BF16), w_o[0].astype(BF16), w_pq[0].astype(BF16))
    pe = _peer(qp, sub_k1[0].astype(BF16), sub_k2[0].astype(BF16), h2,
               w_u[0].astype(BF16), jnp.transpose(w_v[0]).astype(BF16))
    y_p, y_s = _final(x1, pe, mod, g_f.reshape(1, D))

    return (y_p.reshape(1, TP, D), y_s.reshape(NB, SS, D),
            ckv[:TP].reshape(1, 1, TP, KVL), kr[:TP].reshape(1, 1, TP, DR),
            zp[6:8].reshape(1, 1, 2, CW),
            ckv[TP:].reshape(1, NB, SS, KVL), kr[TP:].reshape(1, NB, SS, DR),
            zs.reshape(NB, SS, CW)[:, SS - 2:].reshape(1, NB, 2, CW))
```

```python
import jax
import jax.numpy as jnp
from jax import lax
from jax.experimental import pallas as pl
from jax.experimental.pallas import tpu as pltpu

F32 = jnp.float32
BF16 = jnp.bfloat16

D = 2048
TP = 8192
NB = 32
SS = 16
TS = NB * SS
T = TP + TS
PAST = 1024
CHUNK = 64
NH = 8
DN = 128
DR = 64
DQ = DN + DR
DV = 128
QL = 512
KVL = 512
ROPE_THETA = 10000.0
SCALE = (DN + DR) ** -0.5
CW = 1024
PH = 8
NK = 128
NE = NK * NK
TOPK = 16
EPS = 1e-6
NEG = float(jnp.finfo(jnp.float32).min)

TM = 256
NPT = TP // TM
NT_ = T // TM
MODROWS = 40
PROMPT_ROW = NB

TMP = 512
NI = 8
TE = NI * NK
PCH = 128
VMEM_BIG = 56 * 1024 * 1024

NT_DIMS = (((1,), (1,)), ((), ()))


def _dot(a, b):
    return jnp.dot(a, b, preferred_element_type=F32)


def _dot_nt(a, b):
    return lax.dot_general(a, b, NT_DIMS, preferred_element_type=F32)


def _rms(x, g):
    return x * lax.rsqrt(jnp.mean(x * x, axis=-1, keepdims=True) + EPS) * g


def _group_rows(ref, tile, n_prompt_tiles, groups):
    s = jnp.maximum(tile - n_prompt_tiles, 0)
    rows_s = ref[pl.ds(pl.multiple_of(s * groups, groups), groups), :]
    rows_p = jnp.broadcast_to(ref[PROMPT_ROW:PROMPT_ROW + 1, :], rows_s.shape)
    is_p = jnp.full(rows_s.shape, tile, jnp.int32) < n_prompt_tiles
    return jnp.where(is_p, rows_p, rows_s)


def _per_group(x, fn, *rows):
    n, d = x.shape
    g = rows[0].shape[0]
    x3 = x.reshape(g, n // g, d)
    return fn(x3, *[r[:, None, :] for r in rows]).reshape(n, d)


def _select_tile(tile, n_prompt_tiles, p_ref, s_ref):
    vp = p_ref[...]
    vs = s_ref[...]
    is_p = jnp.full(vp.shape, tile, jnp.int32) < n_prompt_tiles
    return jnp.where(is_p, vp, vs)


def _const(shape):
    nd = len(shape)
    return pl.BlockSpec(shape, lambda *_: (0,) * nd, pipeline_mode=pl.Buffered(1))


def _params(sem, vmem=None):
    return pltpu.CompilerParams(dimension_semantics=sem, vmem_limit_bytes=vmem)


ADA_TN = 1536


def _ada_kernel(c_ref, w_ref, b_ref, o_ref):
    o_ref[...] = _dot(c_ref[...].astype(BF16), w_ref[...].astype(BF16)) + b_ref[...]


def _ada(c_all, w_ada, b_ada):
    n = w_ada.shape[1]
    return pl.pallas_call(
        _ada_kernel,
        out_shape=jax.ShapeDtypeStruct((MODROWS, n), F32),
        grid=(n // ADA_TN,),
        in_specs=[pl.BlockSpec((MODROWS, D), lambda j: (0, 0)),
                  pl.BlockSpec((D, ADA_TN), lambda j: (0, j)),
                  pl.BlockSpec((1, ADA_TN), lambda j: (0, j))],
        out_specs=pl.BlockSpec((MODROWS, ADA_TN), lambda j: (0, j)),
        compiler_params=_params(("arbitrary",), VMEM_BIG),
        name="ada",
    )(c_all, w_ada, b_ada)


def _mod_spec(k):
    return pl.BlockSpec((MODROWS, D), lambda *_: (0, k))


def _tok(width):
    return pl.BlockSpec((TM, width), lambda i: (i, 0))


def _tok_p(width):
    return pl.BlockSpec((TM, width), lambda i: (jnp.minimum(i, NPT - 1), 0))


def _tok_s(width):
    return pl.BlockSpec((TM, width), lambda i: (jnp.maximum(i - NPT, 0), 0))


def _lat_kernel(xp_ref, xs_ref, sh_ref, sc_ref, gn_ref, w_ref, gq_ref, gkv_ref, cos_ref, sin_ref,
                h_ref, cq_ref, ckv_ref, ckvb_ref, kr_ref):
    i = pl.program_id(0)
    x = _select_tile(i, NPT, xp_ref, xs_ref)
    xn = _rms(x, gn_ref[...])
    sh = _group_rows(sh_ref, i, NPT, TM // SS)
    sc = _group_rows(sc_ref, i, NPT, TM // SS)
    hb = _per_group(xn, lambda a, s, c: a * (1 + c) + s, sh, sc).astype(BF16)
    h_ref[...] = hb
    p = _dot(hb, w_ref[...])
    cq_ref[...] = _rms(p[:, :QL], gq_ref[...]).astype(BF16)
    ckv = _rms(p[:, QL:QL + KVL], gkv_ref[...])
    ckv_ref[...] = ckv
    ckvb_ref[...] = ckv.astype(BF16)
    o = QL + KVL
    kr_ref[...] = p[:, o:o + DR] * cos_ref[...] + p[:, o + DR:o + 2 * DR] * sin_ref[...]


def _lat(xp, xs, mod, g_n1, w_lat, g_q, g_kv, cos2, sin2):
    wl = w_lat.shape[1]
    return pl.pallas_call(
        _lat_kernel,
        out_shape=(jax.ShapeDtypeStruct((T, D), BF16),
                   jax.ShapeDtypeStruct((T, QL), BF16),
                   jax.ShapeDtypeStruct((T, KVL), F32),
                   jax.ShapeDtypeStruct((T, KVL), BF16),
                   jax.ShapeDtypeStruct((T, DR), F32)),
        grid=(NT_,),
        in_specs=[_tok_p(D), _tok_s(D), _mod_spec(0), _mod_spec(1), _const((1, D)),
                  _const((D, wl)), _const((1, QL)), _const((1, KVL)), _tok(DR), _tok(DR)],
        out_specs=(_tok(D), _tok(QL), _tok(KVL), _tok(KVL), _tok(DR)),
        compiler_params=_params(("arbitrary",), VMEM_BIG),
        name="lat",
    )(xp, xs, mod, mod, g_n1, w_lat, g_q, g_kv, cos2, sin2)


def _conv_kernel(h_ref, w_ref, wc_ref, bc_ref, s1_ref, s2_ref, u_ref, zs_ref, zp_ref, carry_ref):
    i = pl.program_id(0)

    @pl.when(i == 0)
    def _():
        carry_ref[...] = jnp.zeros_like(carry_ref)

    p = _dot(h_ref[...], w_ref[...])
    z = p[:, 2 * CW:] * p[:, :CW]
    pb = p[:, CW:2 * CW]
    row = lax.broadcasted_iota(jnp.int32, (TM, CW), 0)
    pos = row & (SS - 1)
    is_p = jnp.full((TM, CW), i, jnp.int32) < NPT
    c6 = jnp.broadcast_to(carry_ref[6:7, :], (TM, CW))
    c7 = jnp.broadcast_to(carry_ref[7:8, :], (TM, CW))
    left = jnp.where(is_p, row, pos)
    m1 = left == 0
    m2 = left < 2
    ov1 = jnp.where(is_p, c7, s1_ref[...])
    ov2 = jnp.where(is_p, jnp.where(row == 0, c6, c7), s2_ref[...])
    z1 = jnp.where(m1, ov1, pltpu.roll(z, 1, 0))
    z2 = jnp.where(m2, ov2, pltpu.roll(z, 2, 0))
    yc = wc_ref[0:1, :] * z2 + wc_ref[1:2, :] * z1 + wc_ref[2:3, :] * z + bc_ref[...]
    u_ref[...] = (pb * yc).astype(BF16)
    zs_ref[...] = z
    tail = z[TM - 8:, :]

    @pl.when(i < NPT)
    def _():
        zp_ref[...] = tail

    carry_ref[...] = tail


def _conv(h, w_hbc, w_conv, b_conv, s1, s2):
    return pl.pallas_call(
        _conv_kernel,
        out_shape=(jax.ShapeDtypeStruct((T, CW), BF16),
                   jax.ShapeDtypeStruct((TS, CW), F32),
                   jax.ShapeDtypeStruct((8, CW), F32)),
        grid=(NT_,),
        in_specs=[_tok(D), _const((D, 3 * CW)), _const((3, CW)), _const((1, CW)),
                  _tok_s(CW), _tok_s(CW)],
        out_specs=(_tok(CW), _tok_s(CW), pl.BlockSpec((8, CW), lambda i: (0, 0))),
        scratch_shapes=[pltpu.VMEM((8, CW), F32)],
        compiler_params=_params(("arbitrary",), VMEM_BIG),
        name="conv",
    )(h, w_hbc, w_conv, b_conv, s1, s2)


def _gate_kernel(h_ref, w_ref, g_ref):
    g_ref[...] = jax.nn.sigmoid(_dot(h_ref[...], w_ref[...])).astype(BF16)


def _gate(h, w_g):
    n = w_g.shape[1]
    return pl.pallas_call(
        _gate_kernel,
        out_shape=jax.ShapeDtypeStruct((T, n), BF16),
        grid=(NT_,),
        in_specs=[_tok(D), _const((D, n))],
        out_specs=_tok(n),
        compiler_params=_params(("arbitrary",), VMEM_BIG),
        name="gate",
    )(h, w_g)


QW = DN + 2 * DR


TMA = 512


def _qkv_kernel(cq_ref, ckv_ref, kr_ref, cos_ref, sin_ref, cost_ref, sint_ref,
                wq_ref, wqt_ref, wuk_ref, wuvt_ref, q_ref, qt_ref, k_ref, vt_ref):
    cq = cq_ref[...]
    ckv = ckv_ref[...]
    qf = _dot(cq, wq_ref[...])
    kf = _dot(ckv, wuk_ref[...])
    cos = cos_ref[...]
    sin = sin_ref[...]
    cost = cost_ref[...]
    sint = sint_ref[...]
    krb = kr_ref[...].astype(BF16)
    for h in range(NH):
        o = h * QW
        q_ref[h, :, 0:DN] = qf[:, o:o + DN].astype(BF16)
        q_ref[h, :, DN:DQ] = (qf[:, o + DN:o + DN + DR] * cos
                              + qf[:, o + DN + DR:o + QW] * sin).astype(BF16)
        qt = _dot_nt(wqt_ref[h], cq)
        qt_ref[h, 0:DN, :] = qt[0:DN, :].astype(BF16)
        qt_ref[h, DN:DQ, :] = (qt[DN:DN + DR, :] * cost + qt[DN + DR:QW, :] * sint).astype(BF16)
        k_ref[h, :, 0:DN] = kf[:, h * DN:(h + 1) * DN].astype(BF16)
        k_ref[h, :, DN:DQ] = krb
        vt_ref[h, 0] = _dot_nt(wuvt_ref[h], ckv).astype(BF16)


def _qkv(cq, ckvb, kr, cos2, sin2, cos2t, sin2t, w_q, w_qt, w_uk, w_uvt):
    tok = lambda w: pl.BlockSpec((TMA, w), lambda i: (i, 0))
    tokt = pl.BlockSpec((DR, TMA), lambda i: (0, i))
    return pl.pallas_call(
        _qkv_kernel,
        out_shape=(jax.ShapeDtypeStruct((NH, T, DQ), BF16),
                   jax.ShapeDtypeStruct((NH, DQ, T), BF16),
                   jax.ShapeDtypeStruct((NH, T, DQ), BF16),
                   jax.ShapeDtypeStruct((NH, T // TMA, DV, TMA), BF16)),
        grid=(T // TMA,),
        in_specs=[tok(QL), tok(KVL), tok(DR), tok(DR), tok(DR), tokt, tokt,
                  _const((QL, NH * QW)), _const((NH, QW, QL)), _const((KVL, NH * DN)),
                  _const((NH, DV, KVL))],
        out_specs=(pl.BlockSpec((NH, TMA, DQ), lambda i: (0, i, 0)),
                   pl.BlockSpec((NH, DQ, TMA), lambda i: (0, 0, i)),
                   pl.BlockSpec((NH, TMA, DQ), lambda i: (0, i, 0)),
                   pl.BlockSpec((NH, 1, DV, TMA), lambda i: (0, i, 0, 0))),
        compiler_params=_params(("arbitrary",), VMEM_BIG),
        name="qkv",
    )(cq, ckvb, kr, cos2, sin2, cos2t, sin2t, w_q, w_qt, w_uk, w_uvt)


EXP2_SCALE = SCALE * 1.4426950408889634
AU = 4
AHEAD = 2


def _attn_kernel(qt_ref, k_ref, vt_ref, o_ref, m_ref, l_ref, acc_ref):
    qi = pl.program_id(1)
    qt = qt_ref[0]
    m_ref[...] = jnp.full_like(m_ref, NEG)
    l_ref[...] = jnp.zeros_like(l_ref)
    acc_ref[...] = jnp.zeros_like(acc_ref)

    def steps(tiles):
        m = m_ref[...]
        l = l_ref[...]
        acc = acc_ref[...]

        def scores(j):
            return _dot(k_ref[0, pl.ds(pl.multiple_of(j * TMA, TMA), TMA), :], qt)

        ready = [scores(tiles[t][0]) for t in range(min(AHEAD, len(tiles)))]
        for t, (j, mask) in enumerate(tiles):
            s = ready.pop(0)
            if t + AHEAD < len(tiles):
                ready.append(scores(tiles[t + AHEAD][0]))
            if mask is not None:
                s = jnp.where(mask, s, NEG)
            m_new = jnp.maximum(m, jnp.max(s, axis=0, keepdims=True))
            alpha = jnp.exp2((m - m_new) * EXP2_SCALE)
            p = jnp.exp2((s - m_new) * EXP2_SCALE)
            l = alpha * l + jnp.sum(p, axis=0, keepdims=True)
            acc = alpha * acc + _dot(vt_ref[0, j], p.astype(BF16))
            m = m_new
        m_ref[...] = m
        l_ref[...] = l
        acc_ref[...] = acc

    def body(jj, c):
        steps([(AU * jj + u, None) for u in range(AU)])
        return c

    lax.fori_loop(0, qi // AU, body, 0)
    krow = lax.broadcasted_iota(jnp.int32, (TMA, TMA), 0)
    qcol = lax.broadcasted_iota(jnp.int32, (TMA, TMA), 1)
    diag = (krow // CHUNK) <= (qcol // CHUNK)
    for rem in range(AU):
        @pl.when(qi % AU == rem)
        def _(rem=rem):
            steps([(qi - rem + u, None) for u in range(rem)] + [(qi, diag)])
    o_ref[...] = (acc_ref[...] / l_ref[...]).T.astype(BF16)


def _attn(qt, k, vt):
    return pl.pallas_call(
        _attn_kernel,
        out_shape=jax.ShapeDtypeStruct((TP, NH * DV), BF16),
        grid=(NH, TP // TMA),
        in_specs=[pl.BlockSpec((1, DQ, TMA), lambda h, i: (h, 0, i)),
                  pl.BlockSpec((1, TP, DQ), lambda h, i: (h, 0, 0)),
                  pl.BlockSpec((1, TP // TMA, DV, TMA), lambda h, i: (h, 0, 0, 0))],
        out_specs=pl.BlockSpec((TMA, DV), lambda h, i: (i, h)),
        scratch_shapes=[pltpu.VMEM((1, TMA), F32), pltpu.VMEM((1, TMA), F32),
                        pltpu.VMEM((DV, TMA), F32)],
        compiler_params=_params(("arbitrary", "arbitrary"), VMEM_BIG),
        name="attn",
    )(qt, k, vt)


def _sq_kernel(q_ref, w_ref, o_ref):
    o_ref[0] = _dot(q_ref[0, :, 0:DN], w_ref[0]).astype(BF16)


def _sq(q, w_ukt):
    return pl.pallas_call(
        _sq_kernel,
        out_shape=jax.ShapeDtypeStruct((NH, TS, KVL), BF16),
        grid=(NH,),
        in_specs=[pl.BlockSpec((1, TS, DQ), lambda h: (h, TP // TS, 0)),
                  pl.BlockSpec((1, DN, KVL), lambda h: (h, 0, 0))],
        out_specs=pl.BlockSpec((1, TS, KVL), lambda h: (h, 0, 0)),
        compiler_params=_params(("arbitrary",)),
        name="sq",
    )(q, w_ukt)


def _sattn_kernel(qa_ref, q_ref, cc_ref, ck_ref, nc_ref, nk_ref, o_ref):
    rows = NH * SS
    qa = qa_ref[...].reshape(rows, KVL)
    qr = q_ref[:, :, DN:DQ].reshape(rows, DR)
    cc = cc_ref[0, 0].astype(BF16)
    ck = ck_ref[0, 0].astype(BF16)
    nc = nc_ref[...]
    nk = nk_ref[...].astype(BF16)
    s_c = (_dot_nt(qa, cc) + _dot_nt(qr, ck)) * SCALE
    s_n = (_dot_nt(qa, nc) + _dot_nt(qr, nk)) * SCALE
    qchunk_c = (PAST + (lax.broadcasted_iota(jnp.int32, (rows, PAST), 0) & (SS - 1))) // CHUNK
    qchunk_n = (PAST + (lax.broadcasted_iota(jnp.int32, (rows, SS), 0) & (SS - 1))) // CHUNK
    kchunk_c = lax.broadcasted_iota(jnp.int32, (rows, PAST), 1) // CHUNK
    kchunk_n = (PAST + lax.broadcasted_iota(jnp.int32, (rows, SS), 1)) // CHUNK
    s_c = jnp.where(kchunk_c <= qchunk_c, s_c, NEG)
    s_n = jnp.where(kchunk_n <= qchunk_n, s_n, NEG)
    m = jnp.maximum(jnp.max(s_c, axis=-1, keepdims=True), jnp.max(s_n, axis=-1, keepdims=True))
    p_c = jnp.exp(s_c - m)
    p_n = jnp.exp(s_n - m)
    l = jnp.sum(p_c, axis=-1, keepdims=True) + jnp.sum(p_n, axis=-1, keepdims=True)
    o = (_dot(p_c.astype(BF16), cc) + _dot(p_n.astype(BF16), nc)) / l
    o_ref[...] = o.astype(BF16).reshape(NH, SS, KVL)


def _sattn(q_abs, q, cache_ckv, cache_krope, ckvb, kr):
    nb0 = TP // SS
    return pl.pallas_call(
        _sattn_kernel,
        out_shape=jax.ShapeDtypeStruct((NH, TS, KVL), BF16),
        grid=(NB,),
        in_specs=[pl.BlockSpec((NH, SS, KVL), lambda b: (0, b, 0)),
                  pl.BlockSpec((NH, SS, DQ), lambda b: (0, nb0 + b, 0)),
                  pl.BlockSpec((1, 1, PAST, KVL), lambda b: (0, b, 0, 0)),
                  pl.BlockSpec((1, 1, PAST, DR), lambda b: (0, b, 0, 0)),
                  pl.BlockSpec((SS, KVL), lambda b: (nb0 + b, 0)),
                  pl.BlockSpec((SS, DR), lambda b: (nb0 + b, 0))],
        out_specs=pl.BlockSpec((NH, SS, KVL), lambda b: (0, b, 0)),
        compiler_params=_params(("arbitrary",)),
        name="sattn",
    )(q_abs, q, cache_ckv, cache_krope, ckvb, kr)


def _so_kernel(ol_ref, w_ref, o_ref):
    o_ref[...] = _dot(ol_ref[0], w_ref[...]).astype(BF16)


def _so(o_lat, w_uv):
    return pl.pallas_call(
        _so_kernel,
        out_shape=jax.ShapeDtypeStruct((TS, NH * DV), BF16),
        grid=(NH,),
        in_specs=[pl.BlockSpec((1, TS, KVL), lambda h: (h, 0, 0)),
                  pl.BlockSpec((KVL, DV), lambda h: (0, h))],
        out_specs=pl.BlockSpec((TS, DV), lambda h: (0, h)),
        compiler_params=_params(("arbitrary",)),
        name="so",
    )(o_lat, w_uv)


def _post_kernel(op_ref, os_ref, u_ref, g_ref, xp_ref, xs_ref, gt_ref, sh_ref, sc_ref, gn_ref,
                 woa_ref, wob_ref, wo_ref, wpq_ref, x1_ref, h2t_ref, qp_ref):
    i = pl.program_id(0)
    o = _select_tile(i, NPT, op_ref, os_ref)
    x = _select_tile(i, NPT, xp_ref, xs_ref)
    a = _dot(o, woa_ref[...])
    b = _dot(u_ref[...], wob_ref[...])
    merged = g_ref[:, :D].astype(F32) * a + g_ref[:, D:].astype(F32) * b
    y = _dot(merged.astype(BF16), wo_ref[...])
    groups = TM // SS
    gt = _group_rows(gt_ref, i, NPT, groups)
    x1 = _per_group(y, lambda yy, g: g * yy, gt) + x
    x1_ref[...] = x1
    sh = _group_rows(sh_ref, i, NPT, groups)
    sc = _group_rows(sc_ref, i, NPT, groups)
    h2 = _per_group(_rms(x1, gn_ref[...]), lambda aa, s, c: aa * (1 + c) + s, sh, sc)
    for c in range(D // LG):
        h2t_ref[c * LG:(c + 1) * LG, :] = h2[:, c * LG:(c + 1) * LG].T.astype(BF16)
    qf = _dot(h2.astype(BF16), wpq_ref[...])
    for c in range(2 * PH):
        qp_ref[c] = qf[:, c * NK:(c + 1) * NK].astype(BF16)


def _post(o_p, o_s, u, g, xp, xs, mod, g_n2, w_oa, w_ob, w_o, w_pq):
    return pl.pallas_call(
        _post_kernel,
        out_shape=(jax.ShapeDtypeStruct((T, D), F32),
                   jax.ShapeDtypeStruct((D, T), BF16),
                   jax.ShapeDtypeStruct((2 * PH, T, NK), BF16)),
        grid=(NT_,),
        in_specs=[_tok_p(NH * DV), _tok_s(NH * DV), _tok(CW), _tok(2 * D), _tok_p(D), _tok_s(D),
                  _mod_spec(2), _mod_spec(3), _mod_spec(4), _const((1, D)),
                  _const((NH * DV, D)), _const((CW, D)), _const((D, D)), _const((D, D))],
        out_specs=(_tok(D), pl.BlockSpec((D, TM), lambda i: (0, i)),
                   pl.BlockSpec((2 * PH, TM, NK), lambda i: (0, i, 0))),
        compiler_params=_params(("arbitrary",), VMEM_BIG),
        name="post",
    )(o_p, o_s, u, g, xp, xs, mod, mod, mod, g_n2, w_oa, w_ob, w_o, w_pq)


LG = 128


def _topk_rank(s):
    iota = lax.broadcasted_iota(jnp.int32, s.shape, 0).astype(F32)
    iota16 = lax.broadcasted_iota(jnp.int32, (TOPK, s.shape[1]), 0)
    rank = jnp.full(s.shape, float(TOPK), F32)
    vals = jnp.zeros((TOPK, s.shape[1]), F32)
    for k in range(TOPK):
        m = jnp.max(s, axis=0, keepdims=True)
        idx = jnp.min(jnp.where(s == m, iota, float(NK)), axis=0, keepdims=True)
        hit = iota == idx
        rank = jnp.where(hit, float(k), rank)
        s = jnp.where(hit, -jnp.inf, s)
        vals = jnp.where(iota16 == k, m, vals)
    return vals, rank


def _pair_counts(v1, v2):
    n = v1.shape[1]
    i16 = lax.broadcasted_iota(jnp.int32, (TOPK, n), 0).astype(F32)
    i8 = lax.broadcasted_iota(jnp.int32, (8, n), 0).astype(F32)
    blocks = [v1 + v2[0:1, :]]
    idxs = [i16 * TOPK]
    for b in range(1, 8):
        blocks.append(v1[0:8, :] + v2[b:b + 1, :])
        idxs.append(i8 * TOPK + b)
    blocks.append(v1[0:1, :] + v2[8:16, :])
    idxs.append(i8 + 8.0)
    c = jnp.concatenate(blocks, axis=0)
    ci = jnp.concatenate(idxs, axis=0)
    counts = jnp.zeros((TOPK, n), F32)
    m0 = None
    z = None
    for k in range(TOPK):
        m = jnp.max(c, axis=0, keepdims=True)
        if k == 0:
            m0 = m
            z = jnp.ones_like(m)
        else:
            z = z + jnp.exp(m - m0)
        idx = jnp.min(jnp.where(c == m, ci, float(TOPK * TOPK)), axis=0, keepdims=True)
        c = jnp.where(ci == idx, -jnp.inf, c)
        a_sel = jnp.floor(idx * (1.0 / TOPK))
        counts = counts + jnp.where(i16 == a_sel, 1.0, 0.0)
    return counts, z


def _topk_rank_distinct(s):
    iota16 = lax.broadcasted_iota(jnp.int32, (TOPK, s.shape[1]), 0)
    rank = jnp.full(s.shape, float(TOPK), F32)
    vals = jnp.zeros((TOPK, s.shape[1]), F32)
    for k in range(TOPK):
        m = jnp.max(s, axis=0, keepdims=True)
        hit = s == m
        rank = jnp.where(hit, float(k), rank)
        s = jnp.where(hit, -jnp.inf, s)
        vals = jnp.where(iota16 == k, m, vals)
    taken = jnp.sum(jnp.where(rank < float(TOPK), 1.0, 0.0), axis=0, keepdims=True)
    return vals, rank, taken


def _pair_counts_distinct(v1, v2):
    n = v1.shape[1]
    blocks = [v1 + v2[0:1, :]]
    for b in range(1, 8):
        blocks.append(v1[0:8, :] + v2[b:b + 1, :])
    blocks.append(v1[0:1, :] + v2[8:16, :])
    c = jnp.concatenate(blocks, axis=0)
    m0 = None
    z = None
    for k in range(TOPK):
        m = jnp.max(c, axis=0, keepdims=True)
        if k == 0:
            m0 = m
            z = jnp.ones_like(m)
        else:
            z = z + jnp.exp(m - m0)
        c = jnp.where(c == m, -jnp.inf, c)
    sel = jnp.where(c == -jnp.inf, 1.0, 0.0)
    low = sel[16:24, :]
    for b in range(2, 8):
        low = low + sel[8 + 8 * b:16 + 8 * b, :]
    first = jnp.sum(sel[72:80, :], axis=0, keepdims=True)
    i16 = lax.broadcasted_iota(jnp.int32, (TOPK, n), 0)
    counts = (sel[0:16, :] + jnp.concatenate([low, jnp.zeros((8, n), F32)], axis=0)
              + jnp.where(i16 == 0, first, 0.0))
    return counts, z, jnp.sum(counts, axis=0, keepdims=True)


def _route_finish(s1, s2, v1, r1, v2, r2, counts, z):
    lim = jnp.zeros_like(s1)
    for a in range(TOPK):
        lim = lim + jnp.where(r1 == float(a), counts[a:a + 1, :], 0.0)
    e1 = jnp.exp(s1 - v1[0:1, :])
    e2 = jnp.exp(s2 - v2[0:1, :]) / z
    return r2, e2, lim, e1


def _route_math(s1, s2):
    v1, r1 = _topk_rank(s1)
    v2, r2 = _topk_rank(s2)
    counts, z = _pair_counts(v1, v2)
    return _route_finish(s1, s2, v1, r1, v2, r2, counts, z)


def _route_math_distinct(s1, s2):
    v1, r1, t1 = _topk_rank_distinct(s1)
    v2, r2, t2 = _topk_rank_distinct(s2)
    counts, z, t3 = _pair_counts_distinct(v1, v2)
    full = float(TOPK)
    clean = jnp.where((t1 == full) & (t2 == full) & (t3 == full), 1.0, 0.0)
    return _route_finish(s1, s2, v1, r1, v2, r2, counts, z), clean


SQRT_HALF = 0.7071067811865476
NTP = T // TMP
NET = NK // NI
NG = TMP // LG
UNITS = PH * NG // NET


def _route_scores(qp_ref, k1_ref, k2_ref, h, g):
    off = pl.multiple_of(g * LG, LG)
    s1 = _dot_nt(k1_ref[h], qp_ref[2 * h, pl.ds(off, LG), :])
    s2 = _dot_nt(k2_ref[h], qp_ref[2 * h + 1, pl.ds(off, LG), :])
    return s1, s2


def _route_store(tabs, slot, h, g, r2, e2, lim, e1):
    r2s, e2s, lims, e1s = tabs
    r2s[slot, h, g] = r2.astype(BF16)
    e2s[slot, h, g] = e2.astype(BF16)
    lims[slot, h, g] = lim
    e1s[slot, h, g] = e1


def _gate_act_block(prod, tabs, slot, row0, il, g):
    r2s, e2s, lims, e1s = tabs
    gate = None
    for h in range(PH):
        lim = lims[slot, h, g, pl.ds(row0, NI), :][il:il + 1, :].astype(BF16)
        e1 = e1s[slot, h, g, pl.ds(row0, NI), :][il:il + 1, :].astype(BF16)
        kept = jnp.minimum(e2s[slot, h, g],
                           jnp.maximum(lim - r2s[slot, h, g], jnp.zeros((), BF16)))
        term = e1 * kept
        gate = term if gate is None else gate + term
    half = 0.5 * prod
    act = half + half * lax.erf(prod * SQRT_HALF)
    return (gate.astype(F32) * act).astype(BF16)


def _peer_kernel(qp_ref, k1_ref, k2_ref, h2t_ref, u_ref, vt_ref, o_ref,
                 r2s, e2s, lims, e1s, acc_ref):
    tp = pl.program_id(0)
    e = pl.program_id(1)
    tabs = (r2s, e2s, lims, e1s)
    slot_w = tp % 2
    slot_r = 1 - slot_w
    h = e % PH
    groups = [(e // PH) * UNITS + u for u in range(UNITS)]

    def route_values():
        return [_route_math_distinct(*_route_scores(qp_ref, k1_ref, k2_ref, h, g)) for g in groups]

    def route_finish(results):
        for g, (outs, _) in zip(groups, results):
            _route_store(tabs, slot_w, h, g, *outs)
        for g, (_, clean) in zip(groups, results):
            @pl.when(jnp.min(clean) < 0.5)
            def _(g=g):
                s1, s2 = _route_scores(qp_ref, k1_ref, k2_ref, h, g)
                _route_store(tabs, slot_w, h, g, *_route_math(s1, s2))

    @pl.when(tp == 0)
    def _():
        route_finish(route_values())

    @pl.when(tp > 0)
    def _():
        @pl.when(e == 0)
        def _():
            acc_ref[...] = jnp.zeros_like(acc_ref)

        row0 = pl.multiple_of(e * NI, NI)
        state = [dict() for _ in groups]
        phases = []
        for st, g in zip(state, groups):
            phases += [
                lambda st=st, g=g: st.update(s=_route_scores(qp_ref, k1_ref, k2_ref, h, g)),
                lambda st=st: st.update(a=_topk_rank_distinct(st["s"][0])),
                lambda st=st: st.update(b=_topk_rank_distinct(st["s"][1])),
                lambda st=st: st.update(c=_pair_counts_distinct(st["a"][0], st["b"][0])),
            ]
        pieces = []
        for k in range(TE // PCH):
            prod = _dot(u_ref[k * PCH:(k + 1) * PCH, :], h2t_ref[...])
            if k < len(phases):
                phases[k]()
            for ik in range(PCH // NK):
                pieces.append(jnp.concatenate(
                    [_gate_act_block(prod[ik * NK:(ik + 1) * NK, g * LG:(g + 1) * LG], tabs, slot_r,
                                     row0, k * (PCH // NK) + ik, g)
                     for g in range(NG)], axis=1))
        for ph in phases[TE // PCH:]:
            ph()
        results = []
        for st in state:
            (v1, r1, t1), (v2, r2, t2), (counts, z, t3) = st["a"], st["b"], st["c"]
            full = float(TOPK)
            clean = jnp.where((t1 == full) & (t2 == full) & (t3 == full), 1.0, 0.0)
            results.append((_route_finish(*st["s"], v1, r1, v2, r2, counts, z), clean))
        acc_ref[...] += _dot(vt_ref[...], jnp.concatenate(pieces, axis=0))
        route_finish(results)

        @pl.when(e == NET - 1)
        def _():
            for c in range(D // LG):
                o_ref[:, c * LG:(c + 1) * LG] = acc_ref[c * LG:(c + 1) * LG, :].T


def _peer(qp, k1, k2, h2t, w_u, w_vt):
    tok_r = lambda t: jnp.minimum(t, NTP - 1)
    tok_p = lambda t: jnp.maximum(t - 1, 0)
    return pl.pallas_call(
        _peer_kernel,
        out_shape=jax.ShapeDtypeStruct((T, D), F32),
        grid=(NTP + 1, NET),
        in_specs=[pl.BlockSpec((2 * PH, TMP, NK), lambda t, e: (0, tok_r(t), 0),
                               pipeline_mode=pl.Buffered(1)),
                  _const((PH, NK, NK)), _const((PH, NK, NK)),
                  pl.BlockSpec((D, TMP), lambda t, e: (0, tok_p(t)), pipeline_mode=pl.Buffered(1)),
                  pl.BlockSpec((TE, D), lambda t, e: (jnp.where(t == 0, 0, e), 0)),
                  pl.BlockSpec((D, TE), lambda t, e: (0, jnp.where(t == 0, 0, e)))],
        out_specs=pl.BlockSpec((TMP, D), lambda t, e: (tok_p(t), 0)),
        scratch_shapes=[pltpu.VMEM((2, PH, NG, NK, LG), BF16), pltpu.VMEM((2, PH, NG, NK, LG), BF16),
                        pltpu.VMEM((2, PH, NG, NK, LG), F32), pltpu.VMEM((2, PH, NG, NK, LG), F32),
                        pltpu.VMEM((D, TMP), F32)],
        compiler_params=_params(("arbitrary", "arbitrary"), VMEM_BIG),
        name="peer",
    )(qp, k1, k2, h2t, w_u, w_vt)


def _final_kernel(x1_ref, pe_ref, gt_ref, gf_ref, yp_ref, ys_ref):
    i = pl.program_id(0)
    gt = _group_rows(gt_ref, i, NPT, TM // SS)
    x2 = x1_ref[...] + _per_group(pe_ref[...], lambda pp, g: g * pp, gt)
    y = _rms(x2, gf_ref[...])

    @pl.when(i < NPT)
    def _():
        yp_ref[...] = y

    ys_ref[...] = y


def _final(x1, pe, mod, g_f):
    return pl.pallas_call(
        _final_kernel,
        out_shape=(jax.ShapeDtypeStruct((TP, D), F32), jax.ShapeDtypeStruct((TS, D), F32)),
        grid=(NT_,),
        in_specs=[_tok(D), _tok(D), _mod_spec(5), _const((1, D))],
        out_specs=(_tok_p(D), _tok_s(D)),
        compiler_params=_params(("arbitrary",), VMEM_BIG),
        name="final",
    )(x1, pe, mod, g_f)


def _rot_cols(w):
    half = w.shape[-1] // 2
    return jnp.concatenate([-w[..., half:], w[..., :half]], axis=-1)


def _rope_tables():
    half = DR // 2
    pos = jnp.concatenate([jnp.arange(TP), PAST + jnp.tile(jnp.arange(SS), NB)])
    inv = 1.0 / (ROPE_THETA ** (jnp.arange(half, dtype=F32) / half))
    ang = pos.astype(F32)[:, None] * inv[None, :]
    cos = jnp.cos(ang)
    sin = jnp.sin(ang)
    return jnp.concatenate([cos, cos], axis=1), jnp.concatenate([sin, sin], axis=1)


def kernel(x_prompt, x_sample, cache_ckv, cache_krope, state_conv, c_prompt, c_sample, w_ada, b_ada, g_n1, w_in, g_q, g_kv, w_uq, w_uk, w_uv, w_oa, w_conv, b_conv, w_ob, w_o, g_n2, w_pq, sub_k1, sub_k2, w_u, w_v, g_f):
    assert x_prompt.shape == (1, TP, D) and x_sample.shape == (NB, SS, D)
    assert cache_ckv.shape == (1, NB, PAST, KVL) and w_u.shape == (1, NE, D)
    xp = x_prompt.reshape(TP, D)
    xs = x_sample.reshape(TS, D)
    c_all = jnp.concatenate([c_sample, c_prompt, jnp.zeros((MODROWS - NB - 1, D), F32)], axis=0)
    cos2, sin2 = _rope_tables()

    w = w_in[0]
    o_kr = QL + KVL
    o_h = o_kr + DR
    w_lat = jnp.concatenate([w[:, :o_h], _rot_cols(w[:, o_kr:o_h])], axis=1).astype(BF16)
    w_hbc = w[:, o_h:o_h + 3 * CW].astype(BF16)
    w_g = w[:, o_h + 3 * CW:].astype(BF16)
    wq = w_uq[0]
    w_q3 = jnp.concatenate([wq, _rot_cols(wq[..., DN:])], axis=-1).astype(BF16)
    w_q = w_q3.reshape(QL, NH * QW)
    w_qt = jnp.transpose(w_q3, (1, 2, 0))
    w_uk2 = w_uk[0].reshape(KVL, NH * DN).astype(BF16)
    w_ukt = jnp.transpose(w_uk[0], (1, 2, 0)).astype(BF16)
    w_uv2 = w_uv[0].reshape(KVL, NH * DV).astype(BF16)
    w_uvt = jnp.transpose(w_uv[0], (1, 2, 0)).astype(BF16)
    state = state_conv[0]
    s1 = jnp.pad(state[:, 1:2], ((0, 0), (0, SS - 1), (0, 0))).reshape(TS, CW)
    s2 = jnp.pad(state, ((0, 0), (0, SS - 2), (0, 0))).reshape(TS, CW)

    mod = _ada(c_all, w_ada[0], b_ada)
    h, cq, ckv, ckvb, kr = _lat(xp, xs, mod, g_n1, w_lat, g_q, g_kv, cos2, sin2)
    u, zs, zp = _conv(h, w_hbc, w_conv[0], b_conv, s1, s2)
    g = _gate(h, w_g)
    q, qt, k, vt = _qkv(cq, ckvb, kr, cos2, sin2, cos2.T, sin2.T, w_q, w_qt, w_uk2, w_uvt)
    o_p = _attn(qt, k, vt)
    q_abs = _sq(q, w_ukt)
    o_lat = _sattn(q_abs, q, cache_ckv, cache_krope, ckvb, kr)
    o_s = _so(o_lat, w_uv2)
    x1, h2, qp = _post(o_p, o_s, u, g, xp, xs, mod, g_n2, w_oa[0].astype(BF16),
                       w_ob[0].astype(BF16), w_o[0].astype(BF16), w_pq[0].astype(BF16))
    pe = _peer(qp, sub_k1[0].astype(BF16), sub_k2[0].astype(BF16), h2,
               w_u[0].astype(BF16), jnp.transpose(w_v[0]).astype(BF16))
    y_p, y_s = _final(x1, pe, mod, g_f.reshape(1, D))

    return (y_p.reshape(1, TP, D), y_s.reshape(NB, SS, D),
            ckv[:TP].reshape(1, 1, TP, KVL), kr[:TP].reshape(1, 1, TP, DR),
            zp[6:8].reshape(1, 1, 2, CW),
            ckv[TP:].reshape(1, NB, SS, KVL), kr[TP:].reshape(1, NB, SS, DR),
            zs.reshape(NB, SS, CW)[:, SS - 2:].reshape(1, NB, 2, CW))
```

```python
import jax
import jax.numpy as jnp
from jax import lax
from jax.experimental import pallas as pl
from jax.experimental.pallas import tpu as pltpu

F32 = jnp.float32
BF16 = jnp.bfloat16

D = 2048
TP = 8192
NB = 32
SS = 16
TS = NB * SS
T = TP + TS
PAST = 1024
CHUNK = 64
NH = 8
DN = 128
DR = 64
DQ = DN + DR
DV = 128
QL = 512
KVL = 512
ROPE_THETA = 10000.0
SCALE = (DN + DR) ** -0.5
CW = 1024
PH = 8
NK = 128
NE = NK * NK
TOPK = 16
EPS = 1e-6
NEG = float(jnp.finfo(jnp.float32).min)

TM = 256
NPT = TP // TM
NT_ = T // TM
MODROWS = 40
PROMPT_ROW = NB

TMP = 512
NI = 8
TE = NI * NK
PCH = 128
VMEM_BIG = 56 * 1024 * 1024

NT_DIMS = (((1,), (1,)), ((), ()))


def _dot(a, b):
    return jnp.dot(a, b, preferred_element_type=F32)


def _dot_nt(a, b):
    return lax.dot_general(a, b, NT_DIMS, preferred_element_type=F32)


def _rms(x, g):
    return x * lax.rsqrt(jnp.mean(x * x, axis=-1, keepdims=True) + EPS) * g


def _group_rows(ref, tile, n_prompt_tiles, groups):
    s = jnp.maximum(tile - n_prompt_tiles, 0)
    rows_s = ref[pl.ds(pl.multiple_of(s * groups, groups), groups), :]
    rows_p = jnp.broadcast_to(ref[PROMPT_ROW:PROMPT_ROW + 1, :], rows_s.shape)
    is_p = jnp.full(rows_s.shape, tile, jnp.int32) < n_prompt_tiles
    return jnp.where(is_p, rows_p, rows_s)


def _per_group(x, fn, *rows):
    n, d = x.shape
    g = rows[0].shape[0]
    x3 = x.reshape(g, n // g, d)
    return fn(x3, *[r[:, None, :] for r in rows]).reshape(n, d)


def _select_tile(tile, n_prompt_tiles, p_ref, s_ref):
    vp = p_ref[...]
    vs = s_ref[...]
    is_p = jnp.full(vp.shape, tile, jnp.int32) < n_prompt_tiles
    return jnp.where(is_p, vp, vs)


def _const(shape):
    nd = len(shape)
    return pl.BlockSpec(shape, lambda *_: (0,) * nd, pipeline_mode=pl.Buffered(1))


def _params(sem, vmem=None):
    return pltpu.CompilerParams(dimension_semantics=sem, vmem_limit_bytes=vmem)


ADA_TN = 1536


def _ada_kernel(c_ref, w_ref, b_ref, o_ref):
    o_ref[...] = _dot(c_ref[...].astype(BF16), w_ref[...].astype(BF16)) + b_ref[...]


def _ada(c_all, w_ada, b_ada):
    n = w_ada.shape[1]
    return pl.pallas_call(
        _ada_kernel,
        out_shape=jax.ShapeDtypeStruct((MODROWS, n), F32),
        grid=(n // ADA_TN,),
        in_specs=[pl.BlockSpec((MODROWS, D), lambda j: (0, 0)),
                  pl.BlockSpec((D, ADA_TN), lambda j: (0, j)),
                  pl.BlockSpec((1, ADA_TN), lambda j: (0, j))],
        out_specs=pl.BlockSpec((MODROWS, ADA_TN), lambda j: (0, j)),
        compiler_params=_params(("arbitrary",), VMEM_BIG),
        name="ada",
    )(c_all, w_ada, b_ada)


def _mod_spec(k):
    return pl.BlockSpec((MODROWS, D), lambda *_: (0, k))


def _tok(width):
    return pl.BlockSpec((TM, width), lambda i: (i, 0))


def _tok_p(width):
    return pl.BlockSpec((TM, width), lambda i: (jnp.minimum(i, NPT - 1), 0))


def _tok_s(width):
    return pl.BlockSpec((TM, width), lambda i: (jnp.maximum(i - NPT, 0), 0))


def _lat_kernel(xp_ref, xs_ref, sh_ref, sc_ref, gn_ref, w_ref, gq_ref, gkv_ref, cos_ref, sin_ref,
                h_ref, cq_ref, ckv_ref, ckvb_ref, kr_ref):
    i = pl.program_id(0)
    x = _select_tile(i, NPT, xp_ref, xs_ref)
    xn = _rms(x, gn_ref[...])
    sh = _group_rows(sh_ref, i, NPT, TM // SS)
    sc = _group_rows(sc_ref, i, NPT, TM // SS)
    hb = _per_group(xn, lambda a, s, c: a * (1 + c) + s, sh, sc).astype(BF16)
    h_ref[...] = hb
    p = _dot(hb, w_ref[...])
    cq_ref[...] = _rms(p[:, :QL], gq_ref[...]).astype(BF16)
    ckv = _rms(p[:, QL:QL + KVL], gkv_ref[...])
    ckv_ref[...] = ckv
    ckvb_ref[...] = ckv.astype(BF16)
    o = QL + KVL
    kr_ref[...] = p[:, o:o + DR] * cos_ref[...] + p[:, o + DR:o + 2 * DR] * sin_ref[...]


def _lat(xp, xs, mod, g_n1, w_lat, g_q, g_kv, cos2, sin2):
    wl = w_lat.shape[1]
    return pl.pallas_call(
        _lat_kernel,
        out_shape=(jax.ShapeDtypeStruct((T, D), BF16),
                   jax.ShapeDtypeStruct((T, QL), BF16),
                   jax.ShapeDtypeStruct((T, KVL), F32),
                   jax.ShapeDtypeStruct((T, KVL), BF16),
                   jax.ShapeDtypeStruct((T, DR), F32)),
        grid=(NT_,),
        in_specs=[_tok_p(D), _tok_s(D), _mod_spec(0), _mod_spec(1), _const((1, D)),
                  _const((D, wl)), _const((1, QL)), _const((1, KVL)), _tok(DR), _tok(DR)],
        out_specs=(_tok(D), _tok(QL), _tok(KVL), _tok(KVL), _tok(DR)),
        compiler_params=_params(("arbitrary",), VMEM_BIG),
        name="lat",
    )(xp, xs, mod, mod, g_n1, w_lat, g_q, g_kv, cos2, sin2)


def _conv_kernel(h_ref, w_ref, wc_ref, bc_ref, s1_ref, s2_ref, u_ref, zs_ref, zp_ref, carry_ref):
    i = pl.program_id(0)

    @pl.when(i == 0)
    def _():
        carry_ref[...] = jnp.zeros_like(carry_ref)

    p = _dot(h_ref[...], w_ref[...])
    z = p[:, 2 * CW:] * p[:, :CW]
    pb = p[:, CW:2 * CW]
    row = lax.broadcasted_iota(jnp.int32, (TM, CW), 0)
    pos = row & (SS - 1)
    is_p = jnp.full((TM, CW), i, jnp.int32) < NPT
    c6 = jnp.broadcast_to(carry_ref[6:7, :], (TM, CW))
    c7 = jnp.broadcast_to(carry_ref[7:8, :], (TM, CW))
    left = jnp.where(is_p, row, pos)
    m1 = left == 0
    m2 = left < 2
    ov1 = jnp.where(is_p, c7, s1_ref[...])
    ov2 = jnp.where(is_p, jnp.where(row == 0, c6, c7), s2_ref[...])
    z1 = jnp.where(m1, ov1, pltpu.roll(z, 1, 0))
    z2 = jnp.where(m2, ov2, pltpu.roll(z, 2, 0))
    yc = wc_ref[0:1, :] * z2 + wc_ref[1:2, :] * z1 + wc_ref[2:3, :] * z + bc_ref[...]
    u_ref[...] = (pb * yc).astype(BF16)
    zs_ref[...] = z
    tail = z[TM - 8:, :]

    @pl.when(i < NPT)
    def _():
        zp_ref[...] = tail

    carry_ref[...] = tail


def _conv(h, w_hbc, w_conv, b_conv, s1, s2):
    return pl.pallas_call(
        _conv_kernel,
        out_shape=(jax.ShapeDtypeStruct((T, CW), BF16),
                   jax.ShapeDtypeStruct((TS, CW), F32),
                   jax.ShapeDtypeStruct((8, CW), F32)),
        grid=(NT_,),
        in_specs=[_tok(D), _const((D, 3 * CW)), _const((3, CW)), _const((1, CW)),
                  _tok_s(CW), _tok_s(CW)],
        out_specs=(_tok(CW), _tok_s(CW), pl.BlockSpec((8, CW), lambda i: (0, 0))),
        scratch_shapes=[pltpu.VMEM((8, CW), F32)],
        compiler_params=_params(("arbitrary",), VMEM_BIG),
        name="conv",
    )(h, w_hbc, w_conv, b_conv, s1, s2)


def _gate_kernel(h_ref, w_ref, g_ref):
    g_ref[...] = jax.nn.sigmoid(_dot(h_ref[...], w_ref[...])).astype(BF16)


def _gate(h, w_g):
    n = w_g.shape[1]
    return pl.pallas_call(
        _gate_kernel,
        out_shape=jax.ShapeDtypeStruct((T, n), BF16),
        grid=(NT_,),
        in_specs=[_tok(D), _const((D, n))],
        out_specs=_tok(n),
        compiler_params=_params(("arbitrary",), VMEM_BIG),
        name="gate",
    )(h, w_g)


QW = DN + 2 * DR


TMA = 512


def _qkv_kernel(cq_ref, ckv_ref, kr_ref, cos_ref, sin_ref, cost_ref, sint_ref,
                wq_ref, wqt_ref, wuk_ref, wuvt_ref, q_ref, qt_ref, k_ref, vt_ref):
    cq = cq_ref[...]
    ckv = ckv_ref[...]
    qf = _dot(cq, wq_ref[...])
    kf = _dot(ckv, wuk_ref[...])
    cos = cos_ref[...]
    sin = sin_ref[...]
    cost = cost_ref[...]
    sint = sint_ref[...]
    krb = kr_ref[...].astype(BF16)
    for h in range(NH):
        o = h * QW
        q_ref[h, :, 0:DN] = qf[:, o:o + DN].astype(BF16)
        q_ref[h, :, DN:DQ] = (qf[:, o + DN:o + DN + DR] * cos
                              + qf[:, o + DN + DR:o + QW] * sin).astype(BF16)
        qt = _dot_nt(wqt_ref[h], cq)
        qt_ref[h, 0:DN, :] = qt[0:DN, :].astype(BF16)
        qt_ref[h, DN:DQ, :] = (qt[DN:DN + DR, :] * cost + qt[DN + DR:QW, :] * sint).astype(BF16)
        k_ref[h, :, 0:DN] = kf[:, h * DN:(h + 1) * DN].astype(BF16)
        k_ref[h, :, DN:DQ] = krb
        vt_ref[h, 0] = _dot_nt(wuvt_ref[h], ckv).astype(BF16)


def _qkv(cq, ckvb, kr, cos2, sin2, cos2t, sin2t, w_q, w_qt, w_uk, w_uvt):
    tok = lambda w: pl.BlockSpec((TMA, w), lambda i: (i, 0))
    tokt = pl.BlockSpec((DR, TMA), lambda i: (0, i))
    return pl.pallas_call(
        _qkv_kernel,
        out_shape=(jax.ShapeDtypeStruct((NH, T, DQ), BF16),
                   jax.ShapeDtypeStruct((NH, DQ, T), BF16),
                   jax.ShapeDtypeStruct((NH, T, DQ), BF16),
                   jax.ShapeDtypeStruct((NH, T // TMA, DV, TMA), BF16)),
        grid=(T // TMA,),
        in_specs=[tok(QL), tok(KVL), tok(DR), tok(DR), tok(DR), tokt, tokt,
                  _const((QL, NH * QW)), _const((NH, QW, QL)), _const((KVL, NH * DN)),
                  _const((NH, DV, KVL))],
        out_specs=(pl.BlockSpec((NH, TMA, DQ), lambda i: (0, i, 0)),
                   pl.BlockSpec((NH, DQ, TMA), lambda i: (0, 0, i)),
                   pl.BlockSpec((NH, TMA, DQ), lambda i: (0, i, 0)),
                   pl.BlockSpec((NH, 1, DV, TMA), lambda i: (0, i, 0, 0))),
        compiler_params=_params(("arbitrary",), VMEM_BIG),
        name="qkv",
    )(cq, ckvb, kr, cos2, sin2, cos2t, sin2t, w_q, w_qt, w_uk, w_uvt)


EXP2_SCALE = SCALE * 1.4426950408889634
AU = 4
AHEAD = 2


def _attn_kernel(qt_ref, k_ref, vt_ref, o_ref, m_ref, l_ref, acc_ref):
    qi = pl.program_id(1)
    qt = qt_ref[0]
    m_ref[...] = jnp.full_like(m_ref, NEG)
    l_ref[...] = jnp.zeros_like(l_ref)
    acc_ref[...] = jnp.zeros_like(acc_ref)

    def steps(tiles):
        m = m_ref[...]
        l = l_ref[...]
        acc = acc_ref[...]

        def scores(j):
            return _dot(k_ref[0, pl.ds(pl.multiple_of(j * TMA, TMA), TMA), :], qt)

        ready = [scores(tiles[t][0]) for t in range(min(AHEAD, len(tiles)))]
        for t, (j, mask) in enumerate(tiles):
            s = ready.pop(0)
            if t + AHEAD < len(tiles):
                ready.append(scores(tiles[t + AHEAD][0]))
            if mask is not None:
                s = jnp.where(mask, s, NEG)
            m_new = jnp.maximum(m, jnp.max(s, axis=0, keepdims=True))
            alpha = jnp.exp2((m - m_new) * EXP2_SCALE)
            p = jnp.exp2((s - m_new) * EXP2_SCALE)
            l = alpha * l + jnp.sum(p, axis=0, keepdims=True)
            acc = alpha * acc + _dot(vt_ref[0, j], p.astype(BF16))
            m = m_new
        m_ref[...] = m
        l_ref[...] = l
        acc_ref[...] = acc

    def body(jj, c):
        steps([(AU * jj + u, None) for u in range(AU)])
        return c

    lax.fori_loop(0, qi // AU, body, 0)
    krow = lax.broadcasted_iota(jnp.int32, (TMA, TMA), 0)
    qcol = lax.broadcasted_iota(jnp.int32, (TMA, TMA), 1)
    diag = (krow // CHUNK) <= (qcol // CHUNK)
    for rem in range(AU):
        @pl.when(qi % AU == rem)
        def _(rem=rem):
            steps([(qi - rem + u, None) for u in range(rem)] + [(qi, diag)])
    o_ref[...] = (acc_ref[...] / l_ref[...]).T.astype(BF16)


def _attn(qt, k, vt):
    return pl.pallas_call(
        _attn_kernel,
        out_shape=jax.ShapeDtypeStruct((TP, NH * DV), BF16),
        grid=(NH, TP // TMA),
        in_specs=[pl.BlockSpec((1, DQ, TMA), lambda h, i: (h, 0, i)),
                  pl.BlockSpec((1, TP, DQ), lambda h, i: (h, 0, 0)),
                  pl.BlockSpec((1, TP // TMA, DV, TMA), lambda h, i: (h, 0, 0, 0))],
        out_specs=pl.BlockSpec((TMA, DV), lambda h, i: (i, h)),
        scratch_shapes=[pltpu.VMEM((1, TMA), F32), pltpu.VMEM((1, TMA), F32),
                        pltpu.VMEM((DV, TMA), F32)],
        compiler_params=_params(("arbitrary", "arbitrary"), VMEM_BIG),
        name="attn",
    )(qt, k, vt)


def _sq_kernel(q_ref, w_ref, o_ref):
    o_ref[0] = _dot(q_ref[0, :, 0:DN], w_ref[0]).astype(BF16)


def _sq(q, w_ukt):
    return pl.pallas_call(
        _sq_kernel,
        out_shape=jax.ShapeDtypeStruct((NH, TS, KVL), BF16),
        grid=(NH,),
        in_specs=[pl.BlockSpec((1, TS, DQ), lambda h: (h, TP // TS, 0)),
                  pl.BlockSpec((1, DN, KVL), lambda h: (h, 0, 0))],
        out_specs=pl.BlockSpec((1, TS, KVL), lambda h: (h, 0, 0)),
        compiler_params=_params(("arbitrary",)),
        name="sq",
    )(q, w_ukt)


def _sattn_kernel(qa_ref, q_ref, cc_ref, ck_ref, nc_ref, nk_ref, o_ref):
    rows = NH * SS
    qa = qa_ref[...].reshape(rows, KVL)
    qr = q_ref[:, :, DN:DQ].reshape(rows, DR)
    cc = cc_ref[0, 0].astype(BF16)
    ck = ck_ref[0, 0].astype(BF16)
    nc = nc_ref[...]
    nk = nk_ref[...].astype(BF16)
    s_c = (_dot_nt(qa, cc) + _dot_nt(qr, ck)) * SCALE
    s_n = (_dot_nt(qa, nc) + _dot_nt(qr, nk)) * SCALE
    qchunk_c = (PAST + (lax.broadcasted_iota(jnp.int32, (rows, PAST), 0) & (SS - 1))) // CHUNK
    qchunk_n = (PAST + (lax.broadcasted_iota(jnp.int32, (rows, SS), 0) & (SS - 1))) // CHUNK
    kchunk_c = lax.broadcasted_iota(jnp.int32, (rows, PAST), 1) // CHUNK
    kchunk_n = (PAST + lax.broadcasted_iota(jnp.int32, (rows, SS), 1)) // CHUNK
    s_c = jnp.where(kchunk_c <= qchunk_c, s_c, NEG)
    s_n = jnp.where(kchunk_n <= qchunk_n, s_n, NEG)
    m = jnp.maximum(jnp.max(s_c, axis=-1, keepdims=True), jnp.max(s_n, axis=-1, keepdims=True))
    p_c = jnp.exp(s_c - m)
    p_n = jnp.exp(s_n - m)
    l = jnp.sum(p_c, axis=-1, keepdims=True) + jnp.sum(p_n, axis=-1, keepdims=True)
    o = (_dot(p_c.astype(BF16), cc) + _dot(p_n.astype(BF16), nc)) / l
    o_ref[...] = o.astype(BF16).reshape(NH, SS, KVL)


def _sattn(q_abs, q, cache_ckv, cache_krope, ckvb, kr):
    nb0 = TP // SS
    return pl.pallas_call(
        _sattn_kernel,
        out_shape=jax.ShapeDtypeStruct((NH, TS, KVL), BF16),
        grid=(NB,),
        in_specs=[pl.BlockSpec((NH, SS, KVL), lambda b: (0, b, 0)),
                  pl.BlockSpec((NH, SS, DQ), lambda b: (0, nb0 + b, 0)),
                  pl.BlockSpec((1, 1, PAST, KVL), lambda b: (0, b, 0, 0)),
                  pl.BlockSpec((1, 1, PAST, DR), lambda b: (0, b, 0, 0)),
                  pl.BlockSpec((SS, KVL), lambda b: (nb0 + b, 0)),
                  pl.BlockSpec((SS, DR), lambda b: (nb0 + b, 0))],
        out_specs=pl.BlockSpec((NH, SS, KVL), lambda b: (0, b, 0)),
        compiler_params=_params(("arbitrary",)),
        name="sattn",
    )(q_abs, q, cache_ckv, cache_krope, ckvb, kr)


def _so_kernel(ol_ref, w_ref, o_ref):
    o_ref[...] = _dot(ol_ref[0], w_ref[...]).astype(BF16)


def _so(o_lat, w_uv):
    return pl.pallas_call(
        _so_kernel,
        out_shape=jax.ShapeDtypeStruct((TS, NH * DV), BF16),
        grid=(NH,),
        in_specs=[pl.BlockSpec((1, TS, KVL), lambda h: (h, 0, 0)),
                  pl.BlockSpec((KVL, DV), lambda h: (0, h))],
        out_specs=pl.BlockSpec((TS, DV), lambda h: (0, h)),
        compiler_params=_params(("arbitrary",)),
        name="so",
    )(o_lat, w_uv)


def _post_kernel(op_ref, os_ref, u_ref, g_ref, xp_ref, xs_ref, gt_ref, sh_ref, sc_ref, gn_ref,
                 woa_ref, wob_ref, wo_ref, wpq_ref, x1_ref, h2t_ref, qp_ref):
    i = pl.program_id(0)
    o = _select_tile(i, NPT, op_ref, os_ref)
    x = _select_tile(i, NPT, xp_ref, xs_ref)
    a = _dot(o, woa_ref[...])
    b = _dot(u_ref[...], wob_ref[...])
    merged = g_ref[:, :D].astype(F32) * a + g_ref[:, D:].astype(F32) * b
    y = _dot(merged.astype(BF16), wo_ref[...])
    groups = TM // SS
    gt = _group_rows(gt_ref, i, NPT, groups)
    x1 = _per_group(y, lambda yy, g: g * yy, gt) + x
    x1_ref[...] = x1
    sh = _group_rows(sh_ref, i, NPT, groups)
    sc = _group_rows(sc_ref, i, NPT, groups)
    h2 = _per_group(_rms(x1, gn_ref[...]), lambda aa, s, c: aa * (1 + c) + s, sh, sc)
    for c in range(D // LG):
        h2t_ref[c * LG:(c + 1) * LG, :] = h2[:, c * LG:(c + 1) * LG].T.astype(BF16)
    qf = _dot(h2.astype(BF16), wpq_ref[...])
    for c in range(2 * PH):
        qp_ref[c] = qf[:, c * NK:(c + 1) * NK].astype(BF16)


def _post(o_p, o_s, u, g, xp, xs, mod, g_n2, w_oa, w_ob, w_o, w_pq):
    return pl.pallas_call(
        _post_kernel,
        out_shape=(jax.ShapeDtypeStruct((T, D), F32),
                   jax.ShapeDtypeStruct((D, T), BF16),
                   jax.ShapeDtypeStruct((2 * PH, T, NK), BF16)),
        grid=(NT_,),
        in_specs=[_tok_p(NH * DV), _tok_s(NH * DV), _tok(CW), _tok(2 * D), _tok_p(D), _tok_s(D),
                  _mod_spec(2), _mod_spec(3), _mod_spec(4), _const((1, D)),
                  _const((NH * DV, D)), _const((CW, D)), _const((D, D)), _const((D, D))],
        out_specs=(_tok(D), pl.BlockSpec((D, TM), lambda i: (0, i)),
                   pl.BlockSpec((2 * PH, TM, NK), lambda i: (0, i, 0))),
        compiler_params=_params(("arbitrary",), VMEM_BIG),
        name="post",
    )(o_p, o_s, u, g, xp, xs, mod, mod, mod, g_n2, w_oa, w_ob, w_o, w_pq)


LG = 128


def _topk_rank(s):
    iota = lax.broadcasted_iota(jnp.int32, s.shape, 0).astype(F32)
    iota16 = lax.broadcasted_iota(jnp.int32, (TOPK, s.shape[1]), 0)
    rank = jnp.full(s.shape, float(TOPK), F32)
    vals = jnp.zeros((TOPK, s.shape[1]), F32)
    for k in range(TOPK):
        m = jnp.max(s, axis=0, keepdims=True)
        idx = jnp.min(jnp.where(s == m, iota, float(NK)), axis=0, keepdims=True)
        hit = iota == idx
        rank = jnp.where(hit, float(k), rank)
        s = jnp.where(hit, -jnp.inf, s)
        vals = jnp.where(iota16 == k, m, vals)
    return vals, rank


def _pair_counts(v1, v2):
    n = v1.shape[1]
    i16 = lax.broadcasted_iota(jnp.int32, (TOPK, n), 0).astype(F32)
    i8 = lax.broadcasted_iota(jnp.int32, (8, n), 0).astype(F32)
    blocks = [v1 + v2[0:1, :]]
    idxs = [i16 * TOPK]
    for b in range(1, 8):
        blocks.append(v1[0:8, :] + v2[b:b + 1, :])
        idxs.append(i8 * TOPK + b)
    blocks.append(v1[0:1, :] + v2[8:16, :])
    idxs.append(i8 + 8.0)
    c = jnp.concatenate(blocks, axis=0)
    ci = jnp.concatenate(idxs, axis=0)
    counts = jnp.zeros((TOPK, n), F32)
    m0 = None
    z = None
    for k in range(TOPK):
        m = jnp.max(c, axis=0, keepdims=True)
        if k == 0:
            m0 = m
            z = jnp.ones_like(m)
        else:
            z = z + jnp.exp(m - m0)
        idx = jnp.min(jnp.where(c == m, ci, float(TOPK * TOPK)), axis=0, keepdims=True)
        c = jnp.where(ci == idx, -jnp.inf, c)
        a_sel = jnp.floor(idx * (1.0 / TOPK))
        counts = counts + jnp.where(i16 == a_sel, 1.0, 0.0)
    return counts, z


def _topk_rank_distinct(s):
    iota16 = lax.broadcasted_iota(jnp.int32, (TOPK, s.shape[1]), 0)
    rank = jnp.full(s.shape, float(TOPK), F32)
    vals = jnp.zeros((TOPK, s.shape[1]), F32)
    for k in range(TOPK):
        m = jnp.max(s, axis=0, keepdims=True)
        hit = s == m
        rank = jnp.where(hit, float(k), rank)
        s = jnp.where(hit, -jnp.inf, s)
        vals = jnp.where(iota16 == k, m, vals)
    taken = jnp.sum(jnp.where(rank < float(TOPK), 1.0, 0.0), axis=0, keepdims=True)
    return vals, rank, taken


def _pair_counts_distinct(v1, v2):
    n = v1.shape[1]
    blocks = [v1 + v2[0:1, :]]
    for b in range(1, 8):
        blocks.append(v1[0:8, :] + v2[b:b + 1, :])
    blocks.append(v1[0:1, :] + v2[8:16, :])
    c = jnp.concatenate(blocks, axis=0)
    m0 = None
    z = None
    for k in range(TOPK):
        m = jnp.max(c, axis=0, keepdims=True)
        if k == 0:
            m0 = m
            z = jnp.ones_like(m)
        else:
            z = z + jnp.exp(m - m0)
        c = jnp.where(c == m, -jnp.inf, c)
    sel = jnp.where(c == -jnp.inf, 1.0, 0.0)
    low = sel[16:24, :]
    for b in range(2, 8):
        low = low + sel[8 + 8 * b:16 + 8 * b, :]
    first = jnp.sum(sel[72:80, :], axis=0, keepdims=True)
    i16 = lax.broadcasted_iota(jnp.int32, (TOPK, n), 0)
    counts = (sel[0:16, :] + jnp.concatenate([low, jnp.zeros((8, n), F32)], axis=0)
              + jnp.where(i16 == 0, first, 0.0))
    return counts, z, jnp.sum(counts, axis=0, keepdims=True)


def _route_finish(s1, s2, v1, r1, v2, r2, counts, z):
    lim = jnp.zeros_like(s1)
    for a in range(TOPK):
        lim = lim + jnp.where(r1 == float(a), counts[a:a + 1, :], 0.0)
    e1 = jnp.exp(s1 - v1[0:1, :])
    e2 = jnp.exp(s2 - v2[0:1, :]) / z
    return r2, e2, lim, e1


def _route_math(s1, s2):
    v1, r1 = _topk_rank(s1)
    v2, r2 = _topk_rank(s2)
    counts, z = _pair_counts(v1, v2)
    return _route_finish(s1, s2, v1, r1, v2, r2, counts, z)


def _route_math_distinct(s1, s2):
    v1, r1, t1 = _topk_rank_distinct(s1)
    v2, r2, t2 = _topk_rank_distinct(s2)
    counts, z, t3 = _pair_counts_distinct(v1, v2)
    full = float(TOPK)
    clean = jnp.where((t1 == full) & (t2 == full) & (t3 == full), 1.0, 0.0)
    return _route_finish(s1, s2, v1, r1, v2, r2, counts, z), clean


SQRT_HALF = 0.7071067811865476
NTP = T // TMP
NET = NK // NI
NG = TMP // LG
UNITS = PH * NG // NET


def _route_scores(qp_ref, k1_ref, k2_ref, h, g):
    off = pl.multiple_of(g * LG, LG)
    s1 = _dot_nt(k1_ref[h], qp_ref[2 * h, pl.ds(off, LG), :])
    s2 = _dot_nt(k2_ref[h], qp_ref[2 * h + 1, pl.ds(off, LG), :])
    return s1, s2


def _route_store(tabs, slot, h, g, r2, e2, lim, e1):
    r2s, e2s, lims, e1s = tabs
    r2s[slot, h, g] = r2.astype(BF16)
    e2s[slot, h, g] = e2.astype(BF16)
    lims[slot, h, g] = lim
    e1s[slot, h, g] = e1


def _gate_act_block(prod, tabs, slot, row0, il, g):
    r2s, e2s, lims, e1s = tabs
    gate = None
    for h in range(PH):
        lim = lims[slot, h, g, pl.ds(row0, NI), :][il:il + 1, :].astype(BF16)
        e1 = e1s[slot, h, g, pl.ds(row0, NI), :][il:il + 1, :].astype(BF16)
        kept = jnp.minimum(e2s[slot, h, g],
                           jnp.maximum(lim - r2s[slot, h, g], jnp.zeros((), BF16)))
        term = e1 * kept
        gate = term if gate is None else gate + term
    half = 0.5 * prod
    act = half + half * lax.erf(prod * SQRT_HALF)
    return (gate.astype(F32) * act).astype(BF16)


def _peer_kernel(qp_ref, k1_ref, k2_ref, h2t_ref, u_ref, vt_ref, o_ref,
                 r2s, e2s, lims, e1s, acc_ref):
    tp = pl.program_id(0)
    e = pl.program_id(1)
    tabs = (r2s, e2s, lims, e1s)
    slot_w = tp % 2
    slot_r = 1 - slot_w
    h = e % PH
    groups = [(e // PH) * UNITS + u for u in range(UNITS)]

    def route_values():
        return [_route_math_distinct(*_route_scores(qp_ref, k1_ref, k2_ref, h, g)) for g in groups]

    def route_finish(results):
        for g, (outs, _) in zip(groups, results):
            _route_store(tabs, slot_w, h, g, *outs)
        for g, (_, clean) in zip(groups, results):
            @pl.when(jnp.min(clean) < 0.5)
            def _(g=g):
                s1, s2 = _route_scores(qp_ref, k1_ref, k2_ref, h, g)
                _route_store(tabs, slot_w, h, g, *_route_math(s1, s2))

    @pl.when(tp == 0)
    def _():
        route_finish(route_values())

    @pl.when(tp > 0)
    def _():
        @pl.when(e == 0)
        def _():
            acc_ref[...] = jnp.zeros_like(acc_ref)

        row0 = pl.multiple_of(e * NI, NI)
        results = route_values()
        pieces = []
        for k in range(TE // PCH):
            prod = _dot(u_ref[k * PCH:(k + 1) * PCH, :], h2t_ref[...])
            for ik in range(PCH // NK):
                pieces.append(jnp.concatenate(
                    [_gate_act_block(prod[ik * NK:(ik + 1) * NK, g * LG:(g + 1) * LG], tabs, slot_r,
                                     row0, k * (PCH // NK) + ik, g)
                     for g in range(NG)], axis=1))
        acc_ref[...] += _dot(vt_ref[...], jnp.concatenate(pieces, axis=0))
        route_finish(results)

        @pl.when(e == NET - 1)
        def _():
            for c in range(D // LG):
                o_ref[:, c * LG:(c + 1) * LG] = acc_ref[c * LG:(c + 1) * LG, :].T


def _peer(qp, k1, k2, h2t, w_u, w_vt):
    tok_r = lambda t: jnp.minimum(t, NTP - 1)
    tok_p = lambda t: jnp.maximum(t - 1, 0)
    return pl.pallas_call(
        _peer_kernel,
        out_shape=jax.ShapeDtypeStruct((T, D), F32),
        grid=(NTP + 1, NET),
        in_specs=[pl.BlockSpec((2 * PH, TMP, NK), lambda t, e: (0, tok_r(t), 0),
                               pipeline_mode=pl.Buffered(1)),
                  _const((PH, NK, NK)), _const((PH, NK, NK)),
                  pl.BlockSpec((D, TMP), lambda t, e: (0, tok_p(t)), pipeline_mode=pl.Buffered(1)),
                  pl.BlockSpec((TE, D), lambda t, e: (jnp.where(t == 0, 0, e), 0)),
                  pl.BlockSpec((D, TE), lambda t, e: (0, jnp.where(t == 0, 0, e)))],
        out_specs=pl.BlockSpec((TMP, D), lambda t, e: (tok_p(t), 0)),
        scratch_shapes=[pltpu.VMEM((2, PH, NG, NK, LG), BF16), pltpu.VMEM((2, PH, NG, NK, LG), BF16),
                        pltpu.VMEM((2, PH, NG, NK, LG), F32), pltpu.VMEM((2, PH, NG, NK, LG), F32),
                        pltpu.VMEM((D, TMP), F32)],
        compiler_params=_params(("arbitrary", "arbitrary"), VMEM_BIG),
        name="peer",
    )(qp, k1, k2, h2t, w_u, w_vt)


def _final_kernel(x1_ref, pe_ref, gt_ref, gf_ref, yp_ref, ys_ref):
    i = pl.program_id(0)
    gt = _group_rows(gt_ref, i, NPT, TM // SS)
    x2 = x1_ref[...] + _per_group(pe_ref[...], lambda pp, g: g * pp, gt)
    y = _rms(x2, gf_ref[...])

    @pl.when(i < NPT)
    def _():
        yp_ref[...] = y

    ys_ref[...] = y


def _final(x1, pe, mod, g_f):
    return pl.pallas_call(
        _final_kernel,
        out_shape=(jax.ShapeDtypeStruct((TP, D), F32), jax.ShapeDtypeStruct((TS, D), F32)),
        grid=(NT_,),
        in_specs=[_tok(D), _tok(D), _mod_spec(5), _const((1, D))],
        out_specs=(_tok_p(D), _tok_s(D)),
        compiler_params=_params(("arbitrary",), VMEM_BIG),
        name="final",
    )(x1, pe, mod, g_f)


W_IN_ROWS = 256


def _split_w_in_kernel(w_ref, lat_ref, hbc_ref, g_ref):
    o_kr = QL + KVL
    o_h = o_kr + DR
    half = DR // 2
    lat_ref[:, 0:o_h] = w_ref[:, 0:o_h].astype(BF16)
    lat_ref[:, o_h:o_h + half] = (-w_ref[:, o_kr + half:o_h]).astype(BF16)
    lat_ref[:, o_h + half:o_h + DR] = w_ref[:, o_kr:o_kr + half].astype(BF16)
    hbc_ref[...] = w_ref[:, o_h:o_h + 3 * CW].astype(BF16)
    g_ref[...] = w_ref[:, o_h + 3 * CW:].astype(BF16)


def _split_w_in(w):
    rows, cols = w.shape
    o_h = QL + KVL + DR
    widths = (o_h + DR, 3 * CW, cols - o_h - 3 * CW)
    return pl.pallas_call(
        _split_w_in_kernel,
        out_shape=tuple(jax.ShapeDtypeStruct((rows, n), BF16) for n in widths),
        grid=(rows // W_IN_ROWS,),
        in_specs=[pl.BlockSpec((W_IN_ROWS, cols), lambda i: (i, 0))],
        out_specs=tuple(pl.BlockSpec((W_IN_ROWS, n), lambda i: (i, 0)) for n in widths),
        compiler_params=_params(("arbitrary",), VMEM_BIG),
        name="split_w_in",
    )(w)


def _rot_cols(w):
    half = w.shape[-1] // 2
    return jnp.concatenate([-w[..., half:], w[..., :half]], axis=-1)


def _rope_tables():
    half = DR // 2
    pos = jnp.concatenate([jnp.arange(TP), PAST + jnp.tile(jnp.arange(SS), NB)])
    inv = 1.0 / (ROPE_THETA ** (jnp.arange(half, dtype=F32) / half))
    ang = pos.astype(F32)[:, None] * inv[None, :]
    cos = jnp.cos(ang)
    sin = jnp.sin(ang)
    return jnp.concatenate([cos, cos], axis=1), jnp.concatenate([sin, sin], axis=1)


def kernel(x_prompt, x_sample, cache_ckv, cache_krope, state_conv, c_prompt, c_sample, w_ada, b_ada, g_n1, w_in, g_q, g_kv, w_uq, w_uk, w_uv, w_oa, w_conv, b_conv, w_ob, w_o, g_n2, w_pq, sub_k1, sub_k2, w_u, w_v, g_f):
    assert x_prompt.shape == (1, TP, D) and x_sample.shape == (NB, SS, D)
    assert cache_ckv.shape == (1, NB, PAST, KVL) and w_u.shape == (1, NE, D)
    xp = x_prompt.reshape(TP, D)
    xs = x_sample.reshape(TS, D)
    c_all = jnp.concatenate([c_sample, c_prompt, jnp.zeros((MODROWS - NB - 1, D), F32)], axis=0)
    cos2, sin2 = _rope_tables()

    w_lat, w_hbc, w_g = _split_w_in(w_in[0])
    wq = w_uq[0]
    w_q3 = jnp.concatenate([wq, _rot_cols(wq[..., DN:])], axis=-1).astype(BF16)
    w_q = w_q3.reshape(QL, NH * QW)
    w_qt = jnp.transpose(w_q3, (1, 2, 0))
    w_uk2 = w_uk[0].reshape(KVL, NH * DN).astype(BF16)
    w_ukt = jnp.transpose(w_uk[0], (1, 2, 0)).astype(BF16)
    w_uv2 = w_uv[0].reshape(KVL, NH * DV).astype(BF16)
    w_uvt = jnp.transpose(w_uv[0], (1, 2, 0)).astype(BF16)
    state = state_conv[0]
    s1 = jnp.pad(state[:, 1:2], ((0, 0), (0, SS - 1), (0, 0))).reshape(TS, CW)
    s2 = jnp.pad(state, ((0, 0), (0, SS - 2), (0, 0))).reshape(TS, CW)

    mod = _ada(c_all, w_ada[0], b_ada)
    h, cq, ckv, ckvb, kr = _lat(xp, xs, mod, g_n1, w_lat, g_q, g_kv, cos2, sin2)
    u, zs, zp = _conv(h, w_hbc, w_conv[0], b_conv, s1, s2)
    g = _gate(h, w_g)
    q, qt, k, vt = _qkv(cq, ckvb, kr, cos2, sin2, cos2.T, sin2.T, w_q, w_qt, w_uk2, w_uvt)
    o_p = _attn(qt, k, vt)
    q_abs = _sq(q, w_ukt)
    o_lat = _sattn(q_abs, q, cache_ckv, cache_krope, ckvb, kr)
    o_s = _so(o_lat, w_uv2)
    x1, h2, qp = _post(o_p, o_s, u, g, xp, xs, mod, g_n2, w_oa[0].astype(BF16),
                       w_ob[0].astype(BF16), w_o[0].astype(BF16), w_pq[0].astype(BF16))
    pe = _peer(qp, sub_k1[0].astype(BF16), sub_k2[0].astype(BF16), h2,
               w_u[0].astype(BF16), jnp.transpose(w_v[0]).astype(BF16))
    y_p, y_s = _final(x1, pe, mod, g_f.reshape(1, D))

    return (y_p.reshape(1, TP, D), y_s.reshape(NB, SS, D),
            ckv[:TP].reshape(1, 1, TP, KVL), kr[:TP].reshape(1, 1, TP, DR),
            zp[6:8].reshape(1, 1, 2, CW),
            ckv[TP:].reshape(1, NB, SS, KVL), kr[TP:].reshape(1, NB, SS, DR),
            zs.reshape(NB, SS, CW)[:, SS - 2:].reshape(1, NB, 2, CW))
```

```python
import jax
import jax.numpy as jnp
from jax import lax
from jax.experimental import pallas as pl
from jax.experimental.pallas import tpu as pltpu

F32 = jnp.float32
BF16 = jnp.bfloat16

D = 2048
TP = 8192
NB = 32
SS = 16
TS = NB * SS
T = TP + TS
PAST = 1024
CHUNK = 64
NH = 8
DN = 128
DR = 64
DQ = DN + DR
DV = 128
QL = 512
KVL = 512
ROPE_THETA = 10000.0
SCALE = (DN + DR) ** -0.5
CW = 1024
PH = 8
NK = 128
NE = NK * NK
TOPK = 16
EPS = 1e-6
NEG = float(jnp.finfo(jnp.float32).min)

TM = 256
NPT = TP // TM
NT_ = T // TM
MODROWS = 40
PROMPT_ROW = NB

TMP = 512
NI = 8
TE = NI * NK
PCH = 128
VMEM_BIG = 56 * 1024 * 1024

NT_DIMS = (((1,), (1,)), ((), ()))


def _dot(a, b):
    return jnp.dot(a, b, preferred_element_type=F32)


def _dot_nt(a, b):
    return lax.dot_general(a, b, NT_DIMS, preferred_element_type=F32)


def _rms(x, g):
    return x * lax.rsqrt(jnp.mean(x * x, axis=-1, keepdims=True) + EPS) * g


def _group_rows(ref, tile, n_prompt_tiles, groups):
    s = jnp.maximum(tile - n_prompt_tiles, 0)
    rows_s = ref[pl.ds(pl.multiple_of(s * groups, groups), groups), :]
    rows_p = jnp.broadcast_to(ref[PROMPT_ROW:PROMPT_ROW + 1, :], rows_s.shape)
    is_p = jnp.full(rows_s.shape, tile, jnp.int32) < n_prompt_tiles
    return jnp.where(is_p, rows_p, rows_s)


def _per_group(x, fn, *rows):
    n, d = x.shape
    g = rows[0].shape[0]
    x3 = x.reshape(g, n // g, d)
    return fn(x3, *[r[:, None, :] for r in rows]).reshape(n, d)


def _select_tile(tile, n_prompt_tiles, p_ref, s_ref):
    vp = p_ref[...]
    vs = s_ref[...]
    is_p = jnp.full(vp.shape, tile, jnp.int32) < n_prompt_tiles
    return jnp.where(is_p, vp, vs)


def _const(shape):
    nd = len(shape)
    return pl.BlockSpec(shape, lambda *_: (0,) * nd, pipeline_mode=pl.Buffered(1))


def _params(sem, vmem=None):
    return pltpu.CompilerParams(dimension_semantics=sem, vmem_limit_bytes=vmem)


ADA_TN = 1536


def _ada_kernel(c_ref, w_ref, b_ref, o_ref):
    o_ref[...] = _dot(c_ref[...].astype(BF16), w_ref[...].astype(BF16)) + b_ref[...]


def _ada(c_all, w_ada, b_ada):
    n = w_ada.shape[1]
    return pl.pallas_call(
        _ada_kernel,
        out_shape=jax.ShapeDtypeStruct((MODROWS, n), F32),
        grid=(n // ADA_TN,),
        in_specs=[pl.BlockSpec((MODROWS, D), lambda j: (0, 0)),
                  pl.BlockSpec((D, ADA_TN), lambda j: (0, j)),
                  pl.BlockSpec((1, ADA_TN), lambda j: (0, j))],
        out_specs=pl.BlockSpec((MODROWS, ADA_TN), lambda j: (0, j)),
        compiler_params=_params(("arbitrary",), VMEM_BIG),
        name="ada",
    )(c_all, w_ada, b_ada)


def _mod_spec(k):
    return pl.BlockSpec((MODROWS, D), lambda *_: (0, k))


def _tok(width):
    return pl.BlockSpec((TM, width), lambda i: (i, 0))


def _tok_p(width):
    return pl.BlockSpec((TM, width), lambda i: (jnp.minimum(i, NPT - 1), 0))


def _tok_s(width):
    return pl.BlockSpec((TM, width), lambda i: (jnp.maximum(i - NPT, 0), 0))


TMB = 512
NPTB = TP // TMB
NTB = T // TMB


def _tokb(width):
    return pl.BlockSpec((TMB, width), lambda i: (i, 0))


def _tokb_p(width):
    return pl.BlockSpec((TMB, width), lambda i: (jnp.minimum(i, NPTB - 1), 0))


def _tokb_s(width):
    return pl.BlockSpec((TMB, width), lambda i: (jnp.maximum(i - NPTB, 0), 0))


def _lat_kernel(xp_ref, xs_ref, sh_ref, sc_ref, gn_ref, w_ref, gq_ref, gkv_ref, cos_ref, sin_ref,
                h_ref, cq_ref, ckv_ref, ckvb_ref, kr_ref):
    i = pl.program_id(0)
    x = _select_tile(i, NPTB, xp_ref, xs_ref)
    xn = _rms(x, gn_ref[...])
    sh = _group_rows(sh_ref, i, NPTB, TMB // SS)
    sc = _group_rows(sc_ref, i, NPTB, TMB // SS)
    hb = _per_group(xn, lambda a, s, c: a * (1 + c) + s, sh, sc).astype(BF16)
    h_ref[...] = hb
    p = _dot_nt(hb, w_ref[...])
    cq_ref[...] = _rms(p[:, :QL], gq_ref[...]).astype(BF16)
    ckv = _rms(p[:, QL:QL + KVL], gkv_ref[...])
    ckv_ref[...] = ckv
    ckvb_ref[...] = ckv.astype(BF16)
    o = QL + KVL
    kr_ref[...] = p[:, o:o + DR] * cos_ref[...] + p[:, o + DR:o + 2 * DR] * sin_ref[...]


def _lat(xp, xs, mod, g_n1, w_lat_t, g_q, g_kv, cos2, sin2):
    wl = w_lat_t.shape[0]
    return pl.pallas_call(
        _lat_kernel,
        out_shape=(jax.ShapeDtypeStruct((T, D), BF16),
                   jax.ShapeDtypeStruct((T, QL), BF16),
                   jax.ShapeDtypeStruct((T, KVL), F32),
                   jax.ShapeDtypeStruct((T, KVL), BF16),
                   jax.ShapeDtypeStruct((T, DR), F32)),
        grid=(NTB,),
        in_specs=[_tokb_p(D), _tokb_s(D), _mod_spec(0), _mod_spec(1), _const((1, D)),
                  _const((wl, D)), _const((1, QL)), _const((1, KVL)), _tokb(DR), _tokb(DR)],
        out_specs=(_tokb(D), _tokb(QL), _tokb(KVL), _tokb(KVL), _tokb(DR)),
        compiler_params=_params(("arbitrary",), VMEM_BIG),
        name="lat",
    )(xp, xs, mod, mod, g_n1, w_lat_t, g_q, g_kv, cos2, sin2)


def _conv_kernel(h_ref, w_ref, wc_ref, bc_ref, s1_ref, s2_ref, u_ref, zs_ref, zp_ref, carry_ref):
    i = pl.program_id(0)

    @pl.when(i == 0)
    def _():
        carry_ref[...] = jnp.zeros_like(carry_ref)

    p = _dot_nt(h_ref[...], w_ref[...])
    z = p[:, 2 * CW:] * p[:, :CW]
    pb = p[:, CW:2 * CW]
    row = lax.broadcasted_iota(jnp.int32, (TMB, CW), 0)
    pos = row & (SS - 1)
    is_p = jnp.full((TMB, CW), i, jnp.int32) < NPTB
    c6 = jnp.broadcast_to(carry_ref[6:7, :], (TMB, CW))
    c7 = jnp.broadcast_to(carry_ref[7:8, :], (TMB, CW))
    left = jnp.where(is_p, row, pos)
    m1 = left == 0
    m2 = left < 2
    ov1 = jnp.where(is_p, c7, s1_ref[...])
    ov2 = jnp.where(is_p, jnp.where(row == 0, c6, c7), s2_ref[...])
    z1 = jnp.where(m1, ov1, pltpu.roll(z, 1, 0))
    z2 = jnp.where(m2, ov2, pltpu.roll(z, 2, 0))
    yc = wc_ref[0:1, :] * z2 + wc_ref[1:2, :] * z1 + wc_ref[2:3, :] * z + bc_ref[...]
    u_ref[...] = (pb * yc).astype(BF16)
    zs_ref[...] = z
    tail = z[TMB - 8:, :]

    @pl.when(i < NPTB)
    def _():
        zp_ref[...] = tail

    carry_ref[...] = tail


def _conv(h, w_hbc_t, w_conv, b_conv, s1, s2):
    return pl.pallas_call(
        _conv_kernel,
        out_shape=(jax.ShapeDtypeStruct((T, CW), BF16),
                   jax.ShapeDtypeStruct((TS, CW), F32),
                   jax.ShapeDtypeStruct((8, CW), F32)),
        grid=(NTB,),
        in_specs=[_tokb(D), _const((3 * CW, D)), _const((3, CW)), _const((1, CW)),
                  _tokb_s(CW), _tokb_s(CW)],
        out_specs=(_tokb(CW), _tokb_s(CW), pl.BlockSpec((8, CW), lambda i: (0, 0))),
        scratch_shapes=[pltpu.VMEM((8, CW), F32)],
        compiler_params=_params(("arbitrary",), VMEM_BIG),
        name="conv",
    )(h, w_hbc_t, w_conv, b_conv, s1, s2)


def _gate_kernel(h_ref, w_ref, g_ref):
    g_ref[...] = jax.nn.sigmoid(_dot_nt(h_ref[...], w_ref[...])).astype(BF16)


def _gate(h, w_g_t):
    n = w_g_t.shape[0]
    return pl.pallas_call(
        _gate_kernel,
        out_shape=jax.ShapeDtypeStruct((T, n), BF16),
        grid=(NTB,),
        in_specs=[_tokb(D), _const((n, D))],
        out_specs=_tokb(n),
        compiler_params=_params(("arbitrary",), VMEM_BIG),
        name="gate",
    )(h, w_g_t)


QW = DN + 2 * DR


TMA = 512


def _qkv_kernel(cq_ref, ckv_ref, kr_ref, cos_ref, sin_ref, cost_ref, sint_ref,
                wq_ref, wqt_ref, wuk_ref, wuvt_ref, q_ref, qt_ref, k_ref, vt_ref):
    cq = cq_ref[...]
    ckv = ckv_ref[...]
    qf = _dot(cq, wq_ref[...])
    kf = _dot(ckv, wuk_ref[...])
    cos = cos_ref[...]
    sin = sin_ref[...]
    cost = cost_ref[...]
    sint = sint_ref[...]
    krb = kr_ref[...].astype(BF16)
    for h in range(NH):
        o = h * QW
        q_ref[h, :, 0:DN] = qf[:, o:o + DN].astype(BF16)
        q_ref[h, :, DN:DQ] = (qf[:, o + DN:o + DN + DR] * cos
                              + qf[:, o + DN + DR:o + QW] * sin).astype(BF16)
        qt = _dot_nt(wqt_ref[h], cq)
        qt_ref[h, 0:DN, :] = qt[0:DN, :].astype(BF16)
        qt_ref[h, DN:DQ, :] = (qt[DN:DN + DR, :] * cost + qt[DN + DR:QW, :] * sint).astype(BF16)
        k_ref[h, :, 0:DN] = kf[:, h * DN:(h + 1) * DN].astype(BF16)
        k_ref[h, :, DN:DQ] = krb
        vt_ref[h, 0] = _dot_nt(wuvt_ref[h], ckv).astype(BF16)


def _qkv(cq, ckvb, kr, cos2, sin2, cos2t, sin2t, w_q, w_qt, w_uk, w_uvt):
    tok = lambda w: pl.BlockSpec((TMA, w), lambda i: (i, 0))
    tokt = pl.BlockSpec((DR, TMA), lambda i: (0, i))
    return pl.pallas_call(
        _qkv_kernel,
        out_shape=(jax.ShapeDtypeStruct((NH, T, DQ), BF16),
                   jax.ShapeDtypeStruct((NH, DQ, T), BF16),
                   jax.ShapeDtypeStruct((NH, T, DQ), BF16),
                   jax.ShapeDtypeStruct((NH, T // TMA, DV, TMA), BF16)),
        grid=(T // TMA,),
        in_specs=[tok(QL), tok(KVL), tok(DR), tok(DR), tok(DR), tokt, tokt,
                  _const((QL, NH * QW)), _const((NH, QW, QL)), _const((KVL, NH * DN)),
                  _const((NH, DV, KVL))],
        out_specs=(pl.BlockSpec((NH, TMA, DQ), lambda i: (0, i, 0)),
                   pl.BlockSpec((NH, DQ, TMA), lambda i: (0, 0, i)),
                   pl.BlockSpec((NH, TMA, DQ), lambda i: (0, i, 0)),
                   pl.BlockSpec((NH, 1, DV, TMA), lambda i: (0, i, 0, 0))),
        compiler_params=_params(("arbitrary",), VMEM_BIG),
        name="qkv",
    )(cq, ckvb, kr, cos2, sin2, cos2t, sin2t, w_q, w_qt, w_uk, w_uvt)


EXP2_SCALE = SCALE * 1.4426950408889634
AU = 4
AHEAD = 2


def _attn_kernel(qt_ref, k_ref, vt_ref, o_ref, m_ref, l_ref, acc_ref):
    qi = pl.program_id(1)
    qt = qt_ref[0]
    m_ref[...] = jnp.full_like(m_ref, NEG)
    l_ref[...] = jnp.zeros_like(l_ref)
    acc_ref[...] = jnp.zeros_like(acc_ref)

    def steps(tiles):
        m = m_ref[...]
        l = l_ref[...]
        acc = acc_ref[...]

        def scores(j):
            return _dot(k_ref[0, pl.ds(pl.multiple_of(j * TMA, TMA), TMA), :], qt)

        ready = [scores(tiles[t][0]) for t in range(min(AHEAD, len(tiles)))]
        for t, (j, mask) in enumerate(tiles):
            s = ready.pop(0)
            if t + AHEAD < len(tiles):
                ready.append(scores(tiles[t + AHEAD][0]))
            if mask is not None:
                s = jnp.where(mask, s, NEG)
            m_new = jnp.maximum(m, jnp.max(s, axis=0, keepdims=True))
            alpha = jnp.exp2((m - m_new) * EXP2_SCALE)
            p = jnp.exp2((s - m_new) * EXP2_SCALE)
            l = alpha * l + jnp.sum(p, axis=0, keepdims=True)
            acc = alpha * acc + _dot(vt_ref[0, j], p.astype(BF16))
            m = m_new
        m_ref[...] = m
        l_ref[...] = l
        acc_ref[...] = acc

    def body(jj, c):
        steps([(AU * jj + u, None) for u in range(AU)])
        return c

    lax.fori_loop(0, qi // AU, body, 0)
    krow = lax.broadcasted_iota(jnp.int32, (TMA, TMA), 0)
    qcol = lax.broadcasted_iota(jnp.int32, (TMA, TMA), 1)
    diag = (krow // CHUNK) <= (qcol // CHUNK)
    for rem in range(AU):
        @pl.when(qi % AU == rem)
        def _(rem=rem):
            steps([(qi - rem + u, None) for u in range(rem)] + [(qi, diag)])
    o_ref[...] = (acc_ref[...] / l_ref[...]).T.astype(BF16)


def _attn(qt, k, vt):
    return pl.pallas_call(
        _attn_kernel,
        out_shape=jax.ShapeDtypeStruct((TP, NH * DV), BF16),
        grid=(NH, TP // TMA),
        in_specs=[pl.BlockSpec((1, DQ, TMA), lambda h, i: (h, 0, i)),
                  pl.BlockSpec((1, TP, DQ), lambda h, i: (h, 0, 0)),
                  pl.BlockSpec((1, TP // TMA, DV, TMA), lambda h, i: (h, 0, 0, 0))],
        out_specs=pl.BlockSpec((TMA, DV), lambda h, i: (i, h)),
        scratch_shapes=[pltpu.VMEM((1, TMA), F32), pltpu.VMEM((1, TMA), F32),
                        pltpu.VMEM((DV, TMA), F32)],
        compiler_params=_params(("arbitrary", "arbitrary"), VMEM_BIG),
        name="attn",
    )(qt, k, vt)


def _sq_kernel(q_ref, w_ref, o_ref):
    o_ref[0] = _dot(q_ref[0, :, 0:DN], w_ref[0]).astype(BF16)


def _sq(q, w_ukt):
    return pl.pallas_call(
        _sq_kernel,
        out_shape=jax.ShapeDtypeStruct((NH, TS, KVL), BF16),
        grid=(NH,),
        in_specs=[pl.BlockSpec((1, TS, DQ), lambda h: (h, TP // TS, 0)),
                  pl.BlockSpec((1, DN, KVL), lambda h: (h, 0, 0))],
        out_specs=pl.BlockSpec((1, TS, KVL), lambda h: (h, 0, 0)),
        compiler_params=_params(("arbitrary",)),
        name="sq",
    )(q, w_ukt)


def _sattn_kernel(qa_ref, q_ref, cc_ref, ck_ref, nc_ref, nk_ref, o_ref):
    rows = NH * SS
    qa = qa_ref[...].reshape(rows, KVL)
    qr = q_ref[:, :, DN:DQ].reshape(rows, DR)
    cc = cc_ref[0, 0].astype(BF16)
    ck = ck_ref[0, 0].astype(BF16)
    nc = nc_ref[...]
    nk = nk_ref[...].astype(BF16)
    s_c = (_dot_nt(qa, cc) + _dot(qr, ck)) * SCALE
    s_n = (_dot_nt(qa, nc) + _dot_nt(qr, nk)) * SCALE
    qchunk_c = (PAST + (lax.broadcasted_iota(jnp.int32, (rows, PAST), 0) & (SS - 1))) // CHUNK
    qchunk_n = (PAST + (lax.broadcasted_iota(jnp.int32, (rows, SS), 0) & (SS - 1))) // CHUNK
    kchunk_c = lax.broadcasted_iota(jnp.int32, (rows, PAST), 1) // CHUNK
    kchunk_n = (PAST + lax.broadcasted_iota(jnp.int32, (rows, SS), 1)) // CHUNK
    s_c = jnp.where(kchunk_c <= qchunk_c, s_c, NEG)
    s_n = jnp.where(kchunk_n <= qchunk_n, s_n, NEG)
    m = jnp.maximum(jnp.max(s_c, axis=-1, keepdims=True), jnp.max(s_n, axis=-1, keepdims=True))
    p_c = jnp.exp(s_c - m)
    p_n = jnp.exp(s_n - m)
    l = jnp.sum(p_c, axis=-1, keepdims=True) + jnp.sum(p_n, axis=-1, keepdims=True)
    o = (_dot(p_c.astype(BF16), cc) + _dot(p_n.astype(BF16), nc)) / l
    o_ref[...] = o.astype(BF16).reshape(NH, SS, KVL)


def _sattn(q_abs, q, cache_ckv, cache_krope, ckvb, kr):
    nb0 = TP // SS
    return pl.pallas_call(
        _sattn_kernel,
        out_shape=jax.ShapeDtypeStruct((NH, TS, KVL), BF16),
        grid=(NB,),
        in_specs=[pl.BlockSpec((NH, SS, KVL), lambda b: (0, b, 0)),
                  pl.BlockSpec((NH, SS, DQ), lambda b: (0, nb0 + b, 0)),
                  pl.BlockSpec((1, 1, PAST, KVL), lambda b: (0, b, 0, 0)),
                  pl.BlockSpec((1, 1, DR, PAST), lambda b: (0, b, 0, 0)),
                  pl.BlockSpec((SS, KVL), lambda b: (nb0 + b, 0)),
                  pl.BlockSpec((SS, DR), lambda b: (nb0 + b, 0))],
        out_specs=pl.BlockSpec((NH, SS, KVL), lambda b: (0, b, 0)),
        compiler_params=_params(("arbitrary",)),
        name="sattn",
    )(q_abs, q, cache_ckv, cache_krope, ckvb, kr)


def _so_kernel(ol_ref, w_ref, o_ref):
    o_ref[...] = _dot(ol_ref[0], w_ref[...]).astype(BF16)


def _so(o_lat, w_uv):
    return pl.pallas_call(
        _so_kernel,
        out_shape=jax.ShapeDtypeStruct((TS, NH * DV), BF16),
        grid=(NH,),
        in_specs=[pl.BlockSpec((1, TS, KVL), lambda h: (h, 0, 0)),
                  pl.BlockSpec((KVL, DV), lambda h: (0, h))],
        out_specs=pl.BlockSpec((TS, DV), lambda h: (0, h)),
        compiler_params=_params(("arbitrary",)),
        name="so",
    )(o_lat, w_uv)


def _post_kernel(op_ref, os_ref, u_ref, g_ref, xp_ref, xs_ref, gt_ref, sh_ref, sc_ref, gn_ref,
                 woa_ref, wob_ref, wo_ref, wpq_ref, x1_ref, h2t_ref, qp_ref):
    i = pl.program_id(0)
    o = _select_tile(i, NPT, op_ref, os_ref)
    x = _select_tile(i, NPT, xp_ref, xs_ref)
    a = _dot(o, woa_ref[...])
    b = _dot(u_ref[...], wob_ref[...])
    merged = g_ref[:, :D].astype(F32) * a + g_ref[:, D:].astype(F32) * b
    y = _dot(merged.astype(BF16), wo_ref[...])
    groups = TM // SS
    gt = _group_rows(gt_ref, i, NPT, groups)
    x1 = _per_group(y, lambda yy, g: g * yy, gt) + x
    x1_ref[...] = x1
    sh = _group_rows(sh_ref, i, NPT, groups)
    sc = _group_rows(sc_ref, i, NPT, groups)
    h2 = _per_group(_rms(x1, gn_ref[...]), lambda aa, s, c: aa * (1 + c) + s, sh, sc)
    for c in range(D // LG):
        h2t_ref[c * LG:(c + 1) * LG, :] = h2[:, c * LG:(c + 1) * LG].T.astype(BF16)
    qf = _dot(h2.astype(BF16), wpq_ref[...])
    for c in range(2 * PH):
        qp_ref[c] = qf[:, c * NK:(c + 1) * NK].astype(BF16)


def _post(o_p, o_s, u, g, xp, xs, mod, g_n2, w_oa, w_ob, w_o, w_pq):
    return pl.pallas_call(
        _post_kernel,
        out_shape=(jax.ShapeDtypeStruct((T, D), F32),
                   jax.ShapeDtypeStruct((D, T), BF16),
                   jax.ShapeDtypeStruct((2 * PH, T, NK), BF16)),
        grid=(NT_,),
        in_specs=[_tok_p(NH * DV), _tok_s(NH * DV), _tok(CW), _tok(2 * D), _tok_p(D), _tok_s(D),
                  _mod_spec(2), _mod_spec(3), _mod_spec(4), _const((1, D)),
                  _const((NH * DV, D)), _const((CW, D)), _const((D, D)), _const((D, D))],
        out_specs=(_tok(D), pl.BlockSpec((D, TM), lambda i: (0, i)),
                   pl.BlockSpec((2 * PH, TM, NK), lambda i: (0, i, 0))),
        compiler_params=_params(("arbitrary",), VMEM_BIG),
        name="post",
    )(o_p, o_s, u, g, xp, xs, mod, mod, mod, g_n2, w_oa, w_ob, w_o, w_pq)


LG = 128


def _topk_rank(s):
    iota = lax.broadcasted_iota(jnp.int32, s.shape, 0).astype(F32)
    iota16 = lax.broadcasted_iota(jnp.int32, (TOPK, s.shape[1]), 0)
    rank = jnp.full(s.shape, float(TOPK), F32)
    vals = jnp.zeros((TOPK, s.shape[1]), F32)
    for k in range(TOPK):
        m = jnp.max(s, axis=0, keepdims=True)
        idx = jnp.min(jnp.where(s == m, iota, float(NK)), axis=0, keepdims=True)
        hit = iota == idx
        rank = jnp.where(hit, float(k), rank)
        s = jnp.where(hit, -jnp.inf, s)
        vals = jnp.where(iota16 == k, m, vals)
    return vals, rank


def _pair_counts(v1, v2):
    n = v1.shape[1]
    i16 = lax.broadcasted_iota(jnp.int32, (TOPK, n), 0).astype(F32)
    i8 = lax.broadcasted_iota(jnp.int32, (8, n), 0).astype(F32)
    blocks = [v1 + v2[0:1, :]]
    idxs = [i16 * TOPK]
    for b in range(1, 8):
        blocks.append(v1[0:8, :] + v2[b:b + 1, :])
        idxs.append(i8 * TOPK + b)
    blocks.append(v1[0:1, :] + v2[8:16, :])
    idxs.append(i8 + 8.0)
    c = jnp.concatenate(blocks, axis=0)
    ci = jnp.concatenate(idxs, axis=0)
    counts = jnp.zeros((TOPK, n), F32)
    m0 = None
    z = None
    for k in range(TOPK):
        m = jnp.max(c, axis=0, keepdims=True)
        if k == 0:
            m0 = m
            z = jnp.ones_like(m)
        else:
            z = z + jnp.exp(m - m0)
        idx = jnp.min(jnp.where(c == m, ci, float(TOPK * TOPK)), axis=0, keepdims=True)
        c = jnp.where(ci == idx, -jnp.inf, c)
        a_sel = jnp.floor(idx * (1.0 / TOPK))
        counts = counts + jnp.where(i16 == a_sel, 1.0, 0.0)
    return counts, z


def _topk_rank_distinct(s):
    iota16 = lax.broadcasted_iota(jnp.int32, (TOPK, s.shape[1]), 0)
    rank = jnp.full(s.shape, float(TOPK), F32)
    vals = jnp.zeros((TOPK, s.shape[1]), F32)
    for k in range(TOPK):
        m = jnp.max(s, axis=0, keepdims=True)
        hit = s == m
        rank = jnp.where(hit, float(k), rank)
        s = jnp.where(hit, -jnp.inf, s)
        vals = jnp.where(iota16 == k, m, vals)
    taken = jnp.sum(jnp.where(rank < float(TOPK), 1.0, 0.0), axis=0, keepdims=True)
    return vals, rank, taken


def _pair_counts_distinct(v1, v2):
    n = v1.shape[1]
    blocks = [v1 + v2[0:1, :]]
    for b in range(1, 8):
        blocks.append(v1[0:8, :] + v2[b:b + 1, :])
    blocks.append(v1[0:1, :] + v2[8:16, :])
    c = jnp.concatenate(blocks, axis=0)
    m0 = None
    z = None
    for k in range(TOPK):
        m = jnp.max(c, axis=0, keepdims=True)
        if k == 0:
            m0 = m
            z = jnp.ones_like(m)
        else:
            z = z + jnp.exp(m - m0)
        c = jnp.where(c == m, -jnp.inf, c)
    sel = jnp.where(c == -jnp.inf, 1.0, 0.0)
    low = sel[16:24, :]
    for b in range(2, 8):
        low = low + sel[8 + 8 * b:16 + 8 * b, :]
    first = jnp.sum(sel[72:80, :], axis=0, keepdims=True)
    i16 = lax.broadcasted_iota(jnp.int32, (TOPK, n), 0)
    counts = (sel[0:16, :] + jnp.concatenate([low, jnp.zeros((8, n), F32)], axis=0)
              + jnp.where(i16 == 0, first, 0.0))
    return counts, z, jnp.sum(counts, axis=0, keepdims=True)


def _route_finish(s1, s2, v1, r1, v2, r2, counts, z):
    lim = jnp.zeros_like(s1)
    for a in range(TOPK):
        lim = lim + jnp.where(r1 == float(a), counts[a:a + 1, :], 0.0)
    e1 = jnp.exp(s1 - v1[0:1, :])
    e2 = jnp.exp(s2 - v2[0:1, :]) / z
    return r2, e2, lim, e1


def _route_math(s1, s2):
    v1, r1 = _topk_rank(s1)
    v2, r2 = _topk_rank(s2)
    counts, z = _pair_counts(v1, v2)
    return _route_finish(s1, s2, v1, r1, v2, r2, counts, z)


def _route_math_distinct(s1, s2):
    v1, r1, t1 = _topk_rank_distinct(s1)
    v2, r2, t2 = _topk_rank_distinct(s2)
    counts, z, t3 = _pair_counts_distinct(v1, v2)
    full = float(TOPK)
    clean = jnp.where((t1 == full) & (t2 == full) & (t3 == full), 1.0, 0.0)
    return _route_finish(s1, s2, v1, r1, v2, r2, counts, z), clean


SQRT_HALF = 0.7071067811865476
NTP = T // TMP
NET = NK // NI
NG = TMP // LG
UNITS = PH * NG // NET


def _route_scores(qp_ref, k1_ref, k2_ref, h, g):
    off = pl.multiple_of(g * LG, LG)
    s1 = _dot_nt(k1_ref[h], qp_ref[2 * h, pl.ds(off, LG), :])
    s2 = _dot_nt(k2_ref[h], qp_ref[2 * h + 1, pl.ds(off, LG), :])
    return s1, s2


def _route_store(tabs, slot, h, g, r2, e2, lim, e1):
    r2s, e2s, lims, e1s = tabs
    r2s[slot, h, g] = r2.astype(BF16)
    e2s[slot, h, g] = e2.astype(BF16)
    lims[slot, h, g] = lim
    e1s[slot, h, g] = e1


def _gate_act_block(prod, tabs, slot, row0, il, g):
    r2s, e2s, lims, e1s = tabs
    gate = None
    for h in range(PH):
        lim = lims[slot, h, g, pl.ds(row0, NI), :][il:il + 1, :].astype(BF16)
        e1 = e1s[slot, h, g, pl.ds(row0, NI), :][il:il + 1, :].astype(BF16)
        kept = jnp.minimum(e2s[slot, h, g],
                           jnp.maximum(lim - r2s[slot, h, g], jnp.zeros((), BF16)))
        term = e1 * kept
        gate = term if gate is None else gate + term
    half = 0.5 * prod
    act = half + half * lax.erf(prod * SQRT_HALF)
    return (gate.astype(F32) * act).astype(BF16)


def _peer_kernel(qp_ref, k1_ref, k2_ref, h2t_ref, u_ref, vt_ref, o_ref,
                 r2s, e2s, lims, e1s, acc_ref):
    tp = pl.program_id(0)
    e = pl.program_id(1)
    tabs = (r2s, e2s, lims, e1s)
    slot_w = tp % 2
    slot_r = 1 - slot_w
    h = e % PH
    groups = [(e // PH) * UNITS + u for u in range(UNITS)]

    def route_values():
        return [_route_math_distinct(*_route_scores(qp_ref, k1_ref, k2_ref, h, g)) for g in groups]

    def route_finish(results):
        for g, (outs, _) in zip(groups, results):
            _route_store(tabs, slot_w, h, g, *outs)
        for g, (_, clean) in zip(groups, results):
            @pl.when(jnp.min(clean) < 0.5)
            def _(g=g):
                s1, s2 = _route_scores(qp_ref, k1_ref, k2_ref, h, g)
                _route_store(tabs, slot_w, h, g, *_route_math(s1, s2))

    @pl.when(tp == 0)
    def _():
        route_finish(route_values())

    @pl.when(tp > 0)
    def _():
        @pl.when(e == 0)
        def _():
            acc_ref[...] = jnp.zeros_like(acc_ref)

        row0 = pl.multiple_of(e * NI, NI)
        results = route_values()
        pieces = []
        for k in range(TE // PCH):
            prod = _dot(u_ref[k * PCH:(k + 1) * PCH, :], h2t_ref[...])
            for ik in range(PCH // NK):
                pieces.append(jnp.concatenate(
                    [_gate_act_block(prod[ik * NK:(ik + 1) * NK, g * LG:(g + 1) * LG], tabs, slot_r,
                                     row0, k * (PCH // NK) + ik, g)
                     for g in range(NG)], axis=1))
        acc_ref[...] += _dot(vt_ref[...], jnp.concatenate(pieces, axis=0))
        route_finish(results)

        @pl.when(e == NET - 1)
        def _():
            for c in range(D // LG):
                o_ref[:, c * LG:(c + 1) * LG] = acc_ref[c * LG:(c + 1) * LG, :].T


def _peer(qp, k1, k2, h2t, w_u, w_vt):
    tok_r = lambda t: jnp.minimum(t, NTP - 1)
    tok_p = lambda t: jnp.maximum(t - 1, 0)
    return pl.pallas_call(
        _peer_kernel,
        out_shape=jax.ShapeDtypeStruct((T, D), F32),
        grid=(NTP + 1, NET),
        in_specs=[pl.BlockSpec((2 * PH, TMP, NK), lambda t, e: (0, tok_r(t), 0),
                               pipeline_mode=pl.Buffered(1)),
                  _const((PH, NK, NK)), _const((PH, NK, NK)),
                  pl.BlockSpec((D, TMP), lambda t, e: (0, tok_p(t)), pipeline_mode=pl.Buffered(1)),
                  pl.BlockSpec((TE, D), lambda t, e: (jnp.where(t == 0, 0, e), 0)),
                  pl.BlockSpec((D, TE), lambda t, e: (0, jnp.where(t == 0, 0, e)))],
        out_specs=pl.BlockSpec((TMP, D), lambda t, e: (tok_p(t), 0)),
        scratch_shapes=[pltpu.VMEM((2, PH, NG, NK, LG), BF16), pltpu.VMEM((2, PH, NG, NK, LG), BF16),
                        pltpu.VMEM((2, PH, NG, NK, LG), F32), pltpu.VMEM((2, PH, NG, NK, LG), F32),
                        pltpu.VMEM((D, TMP), F32)],
        compiler_params=_params(("arbitrary", "arbitrary"), VMEM_BIG),
        name="peer",
    )(qp, k1, k2, h2t, w_u, w_vt)


def _final_kernel(x1_ref, pe_ref, gt_ref, gf_ref, yp_ref, ys_ref):
    i = pl.program_id(0)
    gt = _group_rows(gt_ref, i, NPT, TM // SS)
    x2 = x1_ref[...] + _per_group(pe_ref[...], lambda pp, g: g * pp, gt)
    y = _rms(x2, gf_ref[...])

    @pl.when(i < NPT)
    def _():
        yp_ref[...] = y

    ys_ref[...] = y


def _final(x1, pe, mod, g_f):
    return pl.pallas_call(
        _final_kernel,
        out_shape=(jax.ShapeDtypeStruct((TP, D), F32), jax.ShapeDtypeStruct((TS, D), F32)),
        grid=(NT_,),
        in_specs=[_tok(D), _tok(D), _mod_spec(5), _const((1, D))],
        out_specs=(_tok_p(D), _tok_s(D)),
        compiler_params=_params(("arbitrary",), VMEM_BIG),
        name="final",
    )(x1, pe, mod, g_f)


W_IN_COLS = 256


def _split_w_in_kernel(wt_ref, lat_ref, hbc_ref, g_ref):
    o_kr = QL + KVL
    o_h = o_kr + DR
    half = DR // 2
    lat_ref[0:o_h, :] = wt_ref[0:o_h, :].astype(BF16)
    lat_ref[o_h:o_h + half, :] = (-wt_ref[o_kr + half:o_h, :]).astype(BF16)
    lat_ref[o_h + half:o_h + DR, :] = wt_ref[o_kr:o_kr + half, :].astype(BF16)
    hbc_ref[...] = wt_ref[o_h:o_h + 3 * CW, :].astype(BF16)
    g_ref[...] = wt_ref[o_h + 3 * CW:, :].astype(BF16)


def _split_w_in(wt):
    cols, rows = wt.shape
    o_h = QL + KVL + DR
    widths = (o_h + DR, 3 * CW, cols - o_h - 3 * CW)
    return pl.pallas_call(
        _split_w_in_kernel,
        out_shape=tuple(jax.ShapeDtypeStruct((n, rows), BF16) for n in widths),
        grid=(rows // W_IN_COLS,),
        in_specs=[pl.BlockSpec((cols, W_IN_COLS), lambda i: (0, i))],
        out_specs=tuple(pl.BlockSpec((n, W_IN_COLS), lambda i: (0, i)) for n in widths),
        compiler_params=_params(("arbitrary",), VMEM_BIG),
        name="split_w_in",
    )(wt)


def _rot_cols(w):
    half = w.shape[-1] // 2
    return jnp.concatenate([-w[..., half:], w[..., :half]], axis=-1)


def _rope_tables():
    half = DR // 2
    pos = jnp.concatenate([jnp.arange(TP), PAST + jnp.tile(jnp.arange(SS), NB)])
    inv = 1.0 / (ROPE_THETA ** (jnp.arange(half, dtype=F32) / half))
    ang = pos.astype(F32)[:, None] * inv[None, :]
    cos = jnp.cos(ang)
    sin = jnp.sin(ang)
    return jnp.concatenate([cos, cos], axis=1), jnp.concatenate([sin, sin], axis=1)


def kernel(x_prompt, x_sample, cache_ckv, cache_krope, state_conv, c_prompt, c_sample, w_ada, b_ada, g_n1, w_in, g_q, g_kv, w_uq, w_uk, w_uv, w_oa, w_conv, b_conv, w_ob, w_o, g_n2, w_pq, sub_k1, sub_k2, w_u, w_v, g_f):
    assert x_prompt.shape == (1, TP, D) and x_sample.shape == (NB, SS, D)
    assert cache_ckv.shape == (1, NB, PAST, KVL) and w_u.shape == (1, NE, D)
    xp = x_prompt.reshape(TP, D)
    xs = x_sample.reshape(TS, D)
    c_all = jnp.concatenate([c_sample, c_prompt, jnp.zeros((MODROWS - NB - 1, D), F32)], axis=0)
    cos2, sin2 = _rope_tables()

    w_lat, w_hbc, w_g = _split_w_in(w_in[0].T)
    wq = w_uq[0]
    w_q3 = jnp.concatenate([wq, _rot_cols(wq[..., DN:])], axis=-1).astype(BF16)
    w_q = w_q3.reshape(QL, NH * QW)
    w_qt = jnp.transpose(w_q3, (1, 2, 0))
    w_uk2 = w_uk[0].reshape(KVL, NH * DN).astype(BF16)
    w_ukt = jnp.transpose(w_uk[0], (1, 2, 0)).astype(BF16)
    w_uv2 = w_uv[0].reshape(KVL, NH * DV).astype(BF16)
    w_uvt = jnp.transpose(w_uv[0], (1, 2, 0)).astype(BF16)
    state = state_conv[0]
    s1 = jnp.pad(state[:, 1:2], ((0, 0), (0, SS - 1), (0, 0))).reshape(TS, CW)
    s2 = jnp.pad(state, ((0, 0), (0, SS - 2), (0, 0))).reshape(TS, CW)

    mod = _ada(c_all, w_ada[0], b_ada)
    h, cq, ckv, ckvb, kr = _lat(xp, xs, mod, g_n1, w_lat, g_q, g_kv, cos2, sin2)
    u, zs, zp = _conv(h, w_hbc, w_conv[0], b_conv, s1, s2)
    g = _gate(h, w_g)
    q, qt, k, vt = _qkv(cq, ckvb, kr, cos2, sin2, cos2.T, sin2.T, w_q, w_qt, w_uk2, w_uvt)
    o_p = _attn(qt, k, vt)
    q_abs = _sq(q, w_ukt)
    o_lat = _sattn(q_abs, q, cache_ckv, jnp.swapaxes(cache_krope, 2, 3), ckvb, kr)
    o_s = _so(o_lat, w_uv2)
    x1, h2, qp = _post(o_p, o_s, u, g, xp, xs, mod, g_n2, w_oa[0].astype(BF16),
                       w_ob[0].astype(BF16), w_o[0].astype(BF16), w_pq[0].astype(BF16))
    pe = _peer(qp, sub_k1[0].astype(BF16), sub_k2[0].astype(BF16), h2,
               w_u[0].astype(BF16), jnp.transpose(w_v[0]).astype(BF16))
    y_p, y_s = _final(x1, pe, mod, g_f.reshape(1, D))

    return (y_p.reshape(1, TP, D), y_s.reshape(NB, SS, D),
            ckv[:TP].reshape(1, 1, TP, KVL), kr[:TP].reshape(1, 1, TP, DR),
            zp[6:8].reshape(1, 1, 2, CW),
            ckv[TP:].reshape(1, NB, SS, KVL), kr[TP:].reshape(1, NB, SS, DR),
            zs.reshape(NB, SS, CW)[:, SS - 2:].reshape(1, NB, 2, CW))
```

```python
import jax
import jax.numpy as jnp
from jax import lax
from jax.experimental import pallas as pl
from jax.experimental.pallas import tpu as pltpu

F32 = jnp.float32
BF16 = jnp.bfloat16

D = 2048
TP = 8192
NB = 32
SS = 16
TS = NB * SS
T = TP + TS
PAST = 1024
CHUNK = 64
NH = 8
DN = 128
DR = 64
DQ = DN + DR
DV = 128
QL = 512
KVL = 512
ROPE_THETA = 10000.0
SCALE = (DN + DR) ** -0.5
CW = 1024
PH = 8
NK = 128
NE = NK * NK
TOPK = 16
EPS = 1e-6
NEG = float(jnp.finfo(jnp.float32).min)

TM = 256
NPT = TP // TM
NT_ = T // TM
MODROWS = 40
PROMPT_ROW = NB

TMP = 512
NI = 8
TE = NI * NK
PCH = 128
VMEM_BIG = 56 * 1024 * 1024

NT_DIMS = (((1,), (1,)), ((), ()))


def _dot(a, b):
    return jnp.dot(a, b, preferred_element_type=F32)


def _dot_nt(a, b):
    return lax.dot_general(a, b, NT_DIMS, preferred_element_type=F32)


def _rms(x, g):
    return x * lax.rsqrt(jnp.mean(x * x, axis=-1, keepdims=True) + EPS) * g


def _group_rows(ref, tile, n_prompt_tiles, groups):
    s = jnp.maximum(tile - n_prompt_tiles, 0)
    rows_s = ref[pl.ds(pl.multiple_of(s * groups, groups), groups), :]
    rows_p = jnp.broadcast_to(ref[PROMPT_ROW:PROMPT_ROW + 1, :], rows_s.shape)
    is_p = jnp.full(rows_s.shape, tile, jnp.int32) < n_prompt_tiles
    return jnp.where(is_p, rows_p, rows_s)


def _per_group(x, fn, *rows):
    n, d = x.shape
    g = rows[0].shape[0]
    x3 = x.reshape(g, n // g, d)
    return fn(x3, *[r[:, None, :] for r in rows]).reshape(n, d)


def _select_tile(tile, n_prompt_tiles, p_ref, s_ref):
    vp = p_ref[...]
    vs = s_ref[...]
    is_p = jnp.full(vp.shape, tile, jnp.int32) < n_prompt_tiles
    return jnp.where(is_p, vp, vs)


def _const(shape):
    nd = len(shape)
    return pl.BlockSpec(shape, lambda *_: (0,) * nd, pipeline_mode=pl.Buffered(1))


def _params(sem, vmem=None):
    return pltpu.CompilerParams(dimension_semantics=sem, vmem_limit_bytes=vmem)


ADA_TN = 1536


def _ada_kernel(c_ref, w_ref, b_ref, o_ref):
    o_ref[...] = _dot(c_ref[...].astype(BF16), w_ref[...].astype(BF16)) + b_ref[...]


def _ada(c_all, w_ada, b_ada):
    n = w_ada.shape[1]
    return pl.pallas_call(
        _ada_kernel,
        out_shape=jax.ShapeDtypeStruct((MODROWS, n), F32),
        grid=(n // ADA_TN,),
        in_specs=[pl.BlockSpec((MODROWS, D), lambda j: (0, 0)),
                  pl.BlockSpec((D, ADA_TN), lambda j: (0, j)),
                  pl.BlockSpec((1, ADA_TN), lambda j: (0, j))],
        out_specs=pl.BlockSpec((MODROWS, ADA_TN), lambda j: (0, j)),
        compiler_params=_params(("arbitrary",), VMEM_BIG),
        name="ada",
    )(c_all, w_ada, b_ada)


def _mod_spec(k):
    return pl.BlockSpec((MODROWS, D), lambda *_: (0, k))


def _tok(width):
    return pl.BlockSpec((TM, width), lambda i: (i, 0))


def _tok_p(width):
    return pl.BlockSpec((TM, width), lambda i: (jnp.minimum(i, NPT - 1), 0))


def _tok_s(width):
    return pl.BlockSpec((TM, width), lambda i: (jnp.maximum(i - NPT, 0), 0))


TMB = 512
NPTB = TP // TMB
NTB = T // TMB


def _tokb(width):
    return pl.BlockSpec((TMB, width), lambda i: (i, 0))


def _tokb_p(width):
    return pl.BlockSpec((TMB, width), lambda i: (jnp.minimum(i, NPTB - 1), 0))


def _tokb_s(width):
    return pl.BlockSpec((TMB, width), lambda i: (jnp.maximum(i - NPTB, 0), 0))


def _lat_kernel(xp_ref, xs_ref, sh_ref, sc_ref, gn_ref, w_ref, gq_ref, gkv_ref, cos_ref, sin_ref,
                h_ref, cq_ref, ckv_ref, ckvb_ref, kr_ref):
    i = pl.program_id(0)
    x = _select_tile(i, NPTB, xp_ref, xs_ref)
    xn = _rms(x, gn_ref[...])
    sh = _group_rows(sh_ref, i, NPTB, TMB // SS)
    sc = _group_rows(sc_ref, i, NPTB, TMB // SS)
    hb = _per_group(xn, lambda a, s, c: a * (1 + c) + s, sh, sc).astype(BF16)
    h_ref[...] = hb
    p = _dot_nt(hb, w_ref[...])
    cq_ref[...] = _rms(p[:, :QL], gq_ref[...]).astype(BF16)
    ckv = _rms(p[:, QL:QL + KVL], gkv_ref[...])
    ckv_ref[...] = ckv
    ckvb_ref[...] = ckv.astype(BF16)
    o = QL + KVL
    kr_ref[...] = p[:, o:o + DR] * cos_ref[...] + p[:, o + DR:o + 2 * DR] * sin_ref[...]


def _lat(xp, xs, mod, g_n1, w_lat_t, g_q, g_kv, cos2, sin2):
    wl = w_lat_t.shape[0]
    return pl.pallas_call(
        _lat_kernel,
        out_shape=(jax.ShapeDtypeStruct((T, D), BF16),
                   jax.ShapeDtypeStruct((T, QL), BF16),
                   jax.ShapeDtypeStruct((T, KVL), F32),
                   jax.ShapeDtypeStruct((T, KVL), BF16),
                   jax.ShapeDtypeStruct((T, DR), F32)),
        grid=(NTB,),
        in_specs=[_tokb_p(D), _tokb_s(D), _mod_spec(0), _mod_spec(1), _const((1, D)),
                  _const((wl, D)), _const((1, QL)), _const((1, KVL)), _tokb(DR), _tokb(DR)],
        out_specs=(_tokb(D), _tokb(QL), _tokb(KVL), _tokb(KVL), _tokb(DR)),
        compiler_params=_params(("arbitrary",), VMEM_BIG),
        name="lat",
    )(xp, xs, mod, mod, g_n1, w_lat_t, g_q, g_kv, cos2, sin2)


def _conv_kernel(h_ref, w_ref, wc_ref, bc_ref, s1_ref, s2_ref, u_ref, zs_ref, zp_ref, carry_ref):
    i = pl.program_id(0)

    @pl.when(i == 0)
    def _():
        carry_ref[...] = jnp.zeros_like(carry_ref)

    p = _dot_nt(h_ref[...], w_ref[...])
    z = p[:, 2 * CW:] * p[:, :CW]
    pb = p[:, CW:2 * CW]
    row = lax.broadcasted_iota(jnp.int32, (TMB, CW), 0)
    pos = row & (SS - 1)
    is_p = jnp.full((TMB, CW), i, jnp.int32) < NPTB
    c6 = jnp.broadcast_to(carry_ref[6:7, :], (TMB, CW))
    c7 = jnp.broadcast_to(carry_ref[7:8, :], (TMB, CW))
    left = jnp.where(is_p, row, pos)
    m1 = left == 0
    m2 = left < 2
    ov1 = jnp.where(is_p, c7, s1_ref[...])
    ov2 = jnp.where(is_p, jnp.where(row == 0, c6, c7), s2_ref[...])
    z1 = jnp.where(m1, ov1, pltpu.roll(z, 1, 0))
    z2 = jnp.where(m2, ov2, pltpu.roll(z, 2, 0))
    yc = wc_ref[0:1, :] * z2 + wc_ref[1:2, :] * z1 + wc_ref[2:3, :] * z + bc_ref[...]
    u_ref[...] = (pb * yc).astype(BF16)
    zs_ref[...] = z
    tail = z[TMB - 8:, :]

    @pl.when(i < NPTB)
    def _():
        zp_ref[...] = tail

    carry_ref[...] = tail


def _conv(h, w_hbc_t, w_conv, b_conv, s1, s2):
    return pl.pallas_call(
        _conv_kernel,
        out_shape=(jax.ShapeDtypeStruct((T, CW), BF16),
                   jax.ShapeDtypeStruct((TS, CW), F32),
                   jax.ShapeDtypeStruct((8, CW), F32)),
        grid=(NTB,),
        in_specs=[_tokb(D), _const((3 * CW, D)), _const((3, CW)), _const((1, CW)),
                  _tokb_s(CW), _tokb_s(CW)],
        out_specs=(_tokb(CW), _tokb_s(CW), pl.BlockSpec((8, CW), lambda i: (0, 0))),
        scratch_shapes=[pltpu.VMEM((8, CW), F32)],
        compiler_params=_params(("arbitrary",), VMEM_BIG),
        name="conv",
    )(h, w_hbc_t, w_conv, b_conv, s1, s2)


def _gate_kernel(h_ref, w_ref, g_ref):
    g_ref[...] = jax.nn.sigmoid(_dot_nt(h_ref[...], w_ref[...])).astype(BF16)


def _gate(h, w_g_t):
    n = w_g_t.shape[0]
    return pl.pallas_call(
        _gate_kernel,
        out_shape=jax.ShapeDtypeStruct((T, n), BF16),
        grid=(NTB,),
        in_specs=[_tokb(D), _const((n, D))],
        out_specs=_tokb(n),
        compiler_params=_params(("arbitrary",), VMEM_BIG),
        name="gate",
    )(h, w_g_t)


QW = DN + 2 * DR


TMA = 512


def _qkv_kernel(cq_ref, ckv_ref, kr_ref, cos_ref, sin_ref, cost_ref, sint_ref,
                wq_ref, wqt_ref, wuk_ref, wuvt_ref, q_ref, qt_ref, k_ref, vt_ref):
    cq = cq_ref[...]
    ckv = ckv_ref[...]
    qf = _dot(cq, wq_ref[...])
    kf = _dot(ckv, wuk_ref[...])
    cos = cos_ref[...]
    sin = sin_ref[...]
    cost = cost_ref[...]
    sint = sint_ref[...]
    krb = kr_ref[...].astype(BF16)
    for h in range(NH):
        o = h * QW
        q_ref[h, :, 0:DN] = qf[:, o:o + DN].astype(BF16)
        q_ref[h, :, DN:DQ] = (qf[:, o + DN:o + DN + DR] * cos
                              + qf[:, o + DN + DR:o + QW] * sin).astype(BF16)
        qt = _dot_nt(wqt_ref[h], cq)
        qt_ref[h, 0:DN, :] = qt[0:DN, :].astype(BF16)
        qt_ref[h, DN:DQ, :] = (qt[DN:DN + DR, :] * cost + qt[DN + DR:QW, :] * sint).astype(BF16)
        k_ref[h, :, 0:DN] = kf[:, h * DN:(h + 1) * DN].astype(BF16)
        k_ref[h, :, DN:DQ] = krb
        vt_ref[h, 0] = _dot_nt(wuvt_ref[h], ckv).astype(BF16)


def _qkv(cq, ckvb, kr, cos2, sin2, cos2t, sin2t, w_q, w_qt, w_uk, w_uvt):
    tok = lambda w: pl.BlockSpec((TMA, w), lambda i: (i, 0))
    tokt = pl.BlockSpec((DR, TMA), lambda i: (0, i))
    return pl.pallas_call(
        _qkv_kernel,
        out_shape=(jax.ShapeDtypeStruct((NH, T, DQ), BF16),
                   jax.ShapeDtypeStruct((NH, DQ, T), BF16),
                   jax.ShapeDtypeStruct((NH, T, DQ), BF16),
                   jax.ShapeDtypeStruct((NH, T // TMA, DV, TMA), BF16)),
        grid=(T // TMA,),
        in_specs=[tok(QL), tok(KVL), tok(DR), tok(DR), tok(DR), tokt, tokt,
                  _const((QL, NH * QW)), _const((NH, QW, QL)), _const((KVL, NH * DN)),
                  _const((NH, DV, KVL))],
        out_specs=(pl.BlockSpec((NH, TMA, DQ), lambda i: (0, i, 0)),
                   pl.BlockSpec((NH, DQ, TMA), lambda i: (0, 0, i)),
                   pl.BlockSpec((NH, TMA, DQ), lambda i: (0, i, 0)),
                   pl.BlockSpec((NH, 1, DV, TMA), lambda i: (0, i, 0, 0))),
        compiler_params=_params(("arbitrary",), VMEM_BIG),
        name="qkv",
    )(cq, ckvb, kr, cos2, sin2, cos2t, sin2t, w_q, w_qt, w_uk, w_uvt)


EXP2_SCALE = SCALE * 1.4426950408889634
AU = 8
AHEAD = 2


def _attn_kernel(qt_ref, k_ref, vt_ref, o_ref, m_ref, l_ref, acc_ref):
    qi = pl.program_id(1)
    qt = qt_ref[0]
    m_ref[...] = jnp.full_like(m_ref, NEG)
    l_ref[...] = jnp.zeros_like(l_ref)
    acc_ref[...] = jnp.zeros_like(acc_ref)

    def steps(tiles):
        m = m_ref[...]
        l = l_ref[...]
        acc = acc_ref[...]

        def scores(j):
            return _dot(k_ref[0, pl.ds(pl.multiple_of(j * TMA, TMA), TMA), :], qt)

        ready = [scores(tiles[t][0]) for t in range(min(AHEAD, len(tiles)))]
        for t, (j, mask) in enumerate(tiles):
            s = ready.pop(0)
            if t + AHEAD < len(tiles):
                ready.append(scores(tiles[t + AHEAD][0]))
            if mask is not None:
                s = jnp.where(mask, s, NEG)
            m_new = jnp.maximum(m, jnp.max(s, axis=0, keepdims=True))
            alpha = jnp.exp2((m - m_new) * EXP2_SCALE)
            p = jnp.exp2((s - m_new) * EXP2_SCALE)
            l = alpha * l + jnp.sum(p, axis=0, keepdims=True)
            acc = alpha * acc + _dot(vt_ref[0, j], p.astype(BF16))
            m = m_new
        m_ref[...] = m
        l_ref[...] = l
        acc_ref[...] = acc

    def body(jj, c):
        steps([(AU * jj + u, None) for u in range(AU)])
        return c

    lax.fori_loop(0, qi // AU, body, 0)
    krow = lax.broadcasted_iota(jnp.int32, (TMA, TMA), 0)
    qcol = lax.broadcasted_iota(jnp.int32, (TMA, TMA), 1)
    diag = (krow // CHUNK) <= (qcol // CHUNK)
    for rem in range(AU):
        @pl.when(qi % AU == rem)
        def _(rem=rem):
            steps([(qi - rem + u, None) for u in range(rem)] + [(qi, diag)])
    o_ref[...] = (acc_ref[...] / l_ref[...]).T.astype(BF16)


def _attn(qt, k, vt):
    return pl.pallas_call(
        _attn_kernel,
        out_shape=jax.ShapeDtypeStruct((TP, NH * DV), BF16),
        grid=(NH, TP // TMA),
        in_specs=[pl.BlockSpec((1, DQ, TMA), lambda h, i: (h, 0, i)),
                  pl.BlockSpec((1, TP, DQ), lambda h, i: (h, 0, 0)),
                  pl.BlockSpec((1, TP // TMA, DV, TMA), lambda h, i: (h, 0, 0, 0))],
        out_specs=pl.BlockSpec((TMA, DV), lambda h, i: (i, h)),
        scratch_shapes=[pltpu.VMEM((1, TMA), F32), pltpu.VMEM((1, TMA), F32),
                        pltpu.VMEM((DV, TMA), F32)],
        compiler_params=_params(("arbitrary", "arbitrary"), VMEM_BIG),
        name="attn",
    )(qt, k, vt)


def _sq_kernel(q_ref, w_ref, o_ref):
    o_ref[0] = _dot(q_ref[0, :, 0:DN], w_ref[0]).astype(BF16)


def _sq(q, w_ukt):
    return pl.pallas_call(
        _sq_kernel,
        out_shape=jax.ShapeDtypeStruct((NH, TS, KVL), BF16),
        grid=(NH,),
        in_specs=[pl.BlockSpec((1, TS, DQ), lambda h: (h, TP // TS, 0)),
                  pl.BlockSpec((1, DN, KVL), lambda h: (h, 0, 0))],
        out_specs=pl.BlockSpec((1, TS, KVL), lambda h: (h, 0, 0)),
        compiler_params=_params(("arbitrary",)),
        name="sq",
    )(q, w_ukt)


def _sattn_kernel(qa_ref, q_ref, cc_ref, ck_ref, nc_ref, nk_ref, o_ref):
    rows = NH * SS
    qa = qa_ref[...].reshape(rows, KVL)
    qr = q_ref[:, :, DN:DQ].reshape(rows, DR)
    cc = cc_ref[0, 0].astype(BF16)
    ck = ck_ref[0, 0].astype(BF16)
    nc = nc_ref[...]
    nk = nk_ref[...].astype(BF16)
    s_c = (_dot_nt(qa, cc) + _dot(qr, ck)) * SCALE
    s_n = (_dot_nt(qa, nc) + _dot_nt(qr, nk)) * SCALE
    qchunk_c = (PAST + (lax.broadcasted_iota(jnp.int32, (rows, PAST), 0) & (SS - 1))) // CHUNK
    qchunk_n = (PAST + (lax.broadcasted_iota(jnp.int32, (rows, SS), 0) & (SS - 1))) // CHUNK
    kchunk_c = lax.broadcasted_iota(jnp.int32, (rows, PAST), 1) // CHUNK
    kchunk_n = (PAST + lax.broadcasted_iota(jnp.int32, (rows, SS), 1)) // CHUNK
    s_c = jnp.where(kchunk_c <= qchunk_c, s_c, NEG)
    s_n = jnp.where(kchunk_n <= qchunk_n, s_n, NEG)
    m = jnp.maximum(jnp.max(s_c, axis=-1, keepdims=True), jnp.max(s_n, axis=-1, keepdims=True))
    p_c = jnp.exp(s_c - m)
    p_n = jnp.exp(s_n - m)
    l = jnp.sum(p_c, axis=-1, keepdims=True) + jnp.sum(p_n, axis=-1, keepdims=True)
    o = (_dot(p_c.astype(BF16), cc) + _dot(p_n.astype(BF16), nc)) / l
    o_ref[...] = o.astype(BF16).reshape(NH, SS, KVL)


def _sattn(q_abs, q, cache_ckv, cache_krope, ckvb, kr):
    nb0 = TP // SS
    return pl.pallas_call(
        _sattn_kernel,
        out_shape=jax.ShapeDtypeStruct((NH, TS, KVL), BF16),
        grid=(NB,),
        in_specs=[pl.BlockSpec((NH, SS, KVL), lambda b: (0, b, 0)),
                  pl.BlockSpec((NH, SS, DQ), lambda b: (0, nb0 + b, 0)),
                  pl.BlockSpec((1, 1, PAST, KVL), lambda b: (0, b, 0, 0)),
                  pl.BlockSpec((1, 1, DR, PAST), lambda b: (0, b, 0, 0)),
                  pl.BlockSpec((SS, KVL), lambda b: (nb0 + b, 0)),
                  pl.BlockSpec((SS, DR), lambda b: (nb0 + b, 0))],
        out_specs=pl.BlockSpec((NH, SS, KVL), lambda b: (0, b, 0)),
        compiler_params=_params(("arbitrary",)),
        name="sattn",
    )(q_abs, q, cache_ckv, cache_krope, ckvb, kr)


def _so_kernel(ol_ref, w_ref, o_ref):
    o_ref[...] = _dot(ol_ref[0], w_ref[...]).astype(BF16)


def _so(o_lat, w_uv):
    return pl.pallas_call(
        _so_kernel,
        out_shape=jax.ShapeDtypeStruct((TS, NH * DV), BF16),
        grid=(NH,),
        in_specs=[pl.BlockSpec((1, TS, KVL), lambda h: (h, 0, 0)),
                  pl.BlockSpec((KVL, DV), lambda h: (0, h))],
        out_specs=pl.BlockSpec((TS, DV), lambda h: (0, h)),
        compiler_params=_params(("arbitrary",)),
        name="so",
    )(o_lat, w_uv)


def _post_kernel(op_ref, os_ref, u_ref, g_ref, xp_ref, xs_ref, gt_ref, sh_ref, sc_ref, gn_ref,
                 woa_ref, wob_ref, wo_ref, wpq_ref, x1_ref, h2t_ref, qp_ref):
    i = pl.program_id(0)
    o = _select_tile(i, NPT, op_ref, os_ref)
    x = _select_tile(i, NPT, xp_ref, xs_ref)
    a = _dot(o, woa_ref[...])
    b = _dot(u_ref[...], wob_ref[...])
    merged = g_ref[:, :D].astype(F32) * a + g_ref[:, D:].astype(F32) * b
    y = _dot(merged.astype(BF16), wo_ref[...])
    groups = TM // SS
    gt = _group_rows(gt_ref, i, NPT, groups)
    x1 = _per_group(y, lambda yy, g: g * yy, gt) + x
    x1_ref[...] = x1
    sh = _group_rows(sh_ref, i, NPT, groups)
    sc = _group_rows(sc_ref, i, NPT, groups)
    h2 = _per_group(_rms(x1, gn_ref[...]), lambda aa, s, c: aa * (1 + c) + s, sh, sc)
    for c in range(D // LG):
        h2t_ref[c * LG:(c + 1) * LG, :] = h2[:, c * LG:(c + 1) * LG].T.astype(BF16)
    qf = _dot(h2.astype(BF16), wpq_ref[...])
    for c in range(2 * PH):
        qp_ref[c] = qf[:, c * NK:(c + 1) * NK].astype(BF16)


def _post(o_p, o_s, u, g, xp, xs, mod, g_n2, w_oa, w_ob, w_o, w_pq):
    return pl.pallas_call(
        _post_kernel,
        out_shape=(jax.ShapeDtypeStruct((T, D), F32),
                   jax.ShapeDtypeStruct((D, T), BF16),
                   jax.ShapeDtypeStruct((2 * PH, T, NK), BF16)),
        grid=(NT_,),
        in_specs=[_tok_p(NH * DV), _tok_s(NH * DV), _tok(CW), _tok(2 * D), _tok_p(D), _tok_s(D),
                  _mod_spec(2), _mod_spec(3), _mod_spec(4), _const((1, D)),
                  _const((NH * DV, D)), _const((CW, D)), _const((D, D)), _const((D, D))],
        out_specs=(_tok(D), pl.BlockSpec((D, TM), lambda i: (0, i)),
                   pl.BlockSpec((2 * PH, TM, NK), lambda i: (0, i, 0))),
        compiler_params=_params(("arbitrary",), VMEM_BIG),
        name="post",
    )(o_p, o_s, u, g, xp, xs, mod, mod, mod, g_n2, w_oa, w_ob, w_o, w_pq)


LG = 128


def _topk_rank(s):
    iota = lax.broadcasted_iota(jnp.int32, s.shape, 0).astype(F32)
    iota16 = lax.broadcasted_iota(jnp.int32, (TOPK, s.shape[1]), 0)
    rank = jnp.full(s.shape, float(TOPK), F32)
    vals = jnp.zeros((TOPK, s.shape[1]), F32)
    for k in range(TOPK):
        m = jnp.max(s, axis=0, keepdims=True)
        idx = jnp.min(jnp.where(s == m, iota, float(NK)), axis=0, keepdims=True)
        hit = iota == idx
        rank = jnp.where(hit, float(k), rank)
        s = jnp.where(hit, -jnp.inf, s)
        vals = jnp.where(iota16 == k, m, vals)
    return vals, rank


def _pair_counts(v1, v2):
    n = v1.shape[1]
    i16 = lax.broadcasted_iota(jnp.int32, (TOPK, n), 0).astype(F32)
    i8 = lax.broadcasted_iota(jnp.int32, (8, n), 0).astype(F32)
    blocks = [v1 + v2[0:1, :]]
    idxs = [i16 * TOPK]
    for b in range(1, 8):
        blocks.append(v1[0:8, :] + v2[b:b + 1, :])
        idxs.append(i8 * TOPK + b)
    blocks.append(v1[0:1, :] + v2[8:16, :])
    idxs.append(i8 + 8.0)
    c = jnp.concatenate(blocks, axis=0)
    ci = jnp.concatenate(idxs, axis=0)
    counts = jnp.zeros((TOPK, n), F32)
    m0 = None
    z = None
    for k in range(TOPK):
        m = jnp.max(c, axis=0, keepdims=True)
        if k == 0:
            m0 = m
            z = jnp.ones_like(m)
        else:
            z = z + jnp.exp(m - m0)
        idx = jnp.min(jnp.where(c == m, ci, float(TOPK * TOPK)), axis=0, keepdims=True)
        c = jnp.where(ci == idx, -jnp.inf, c)
        a_sel = jnp.floor(idx * (1.0 / TOPK))
        counts = counts + jnp.where(i16 == a_sel, 1.0, 0.0)
    return counts, z


def _topk_rank_distinct(s):
    iota16 = lax.broadcasted_iota(jnp.int32, (TOPK, s.shape[1]), 0)
    rank = jnp.full(s.shape, float(TOPK), F32)
    vals = jnp.zeros((TOPK, s.shape[1]), F32)
    for k in range(TOPK):
        m = jnp.max(s, axis=0, keepdims=True)
        hit = s == m
        rank = jnp.where(hit, float(k), rank)
        s = jnp.where(hit, -jnp.inf, s)
        vals = jnp.where(iota16 == k, m, vals)
    taken = jnp.sum(jnp.where(rank < float(TOPK), 1.0, 0.0), axis=0, keepdims=True)
    return vals, rank, taken


def _pair_counts_distinct(v1, v2):
    n = v1.shape[1]
    blocks = [v1 + v2[0:1, :]]
    for b in range(1, 8):
        blocks.append(v1[0:8, :] + v2[b:b + 1, :])
    blocks.append(v1[0:1, :] + v2[8:16, :])
    c = jnp.concatenate(blocks, axis=0)
    m0 = None
    z = None
    for k in range(TOPK):
        m = jnp.max(c, axis=0, keepdims=True)
        if k == 0:
            m0 = m
            z = jnp.ones_like(m)
        else:
            z = z + jnp.exp(m - m0)
        c = jnp.where(c == m, -jnp.inf, c)
    sel = jnp.where(c == -jnp.inf, 1.0, 0.0)
    low = sel[16:24, :]
    for b in range(2, 8):
        low = low + sel[8 + 8 * b:16 + 8 * b, :]
    first = jnp.sum(sel[72:80, :], axis=0, keepdims=True)
    i16 = lax.broadcasted_iota(jnp.int32, (TOPK, n), 0)
    counts = (sel[0:16, :] + jnp.concatenate([low, jnp.zeros((8, n), F32)], axis=0)
              + jnp.where(i16 == 0, first, 0.0))
    return counts, z, jnp.sum(counts, axis=0, keepdims=True)


def _route_finish(s1, s2, v1, r1, v2, r2, counts, z):
    lim = jnp.zeros_like(s1)
    for a in range(TOPK):
        lim = lim + jnp.where(r1 == float(a), counts[a:a + 1, :], 0.0)
    e1 = jnp.exp(s1 - v1[0:1, :])
    e2 = jnp.exp(s2 - v2[0:1, :]) / z
    return r2, e2, lim, e1


def _route_math(s1, s2):
    v1, r1 = _topk_rank(s1)
    v2, r2 = _topk_rank(s2)
    counts, z = _pair_counts(v1, v2)
    return _route_finish(s1, s2, v1, r1, v2, r2, counts, z)


def _route_math_distinct(s1, s2):
    v1, r1, t1 = _topk_rank_distinct(s1)
    v2, r2, t2 = _topk_rank_distinct(s2)
    counts, z, t3 = _pair_counts_distinct(v1, v2)
    full = float(TOPK)
    clean = jnp.where((t1 == full) & (t2 == full) & (t3 == full), 1.0, 0.0)
    return _route_finish(s1, s2, v1, r1, v2, r2, counts, z), clean


SQRT_HALF = 0.7071067811865476
NTP = T // TMP
NET = NK // NI
NG = TMP // LG
UNITS = PH * NG // NET


def _route_scores(qp_ref, k1_ref, k2_ref, h, g):
    off = pl.multiple_of(g * LG, LG)
    s1 = _dot_nt(k1_ref[h], qp_ref[2 * h, pl.ds(off, LG), :])
    s2 = _dot_nt(k2_ref[h], qp_ref[2 * h + 1, pl.ds(off, LG), :])
    return s1, s2


def _route_store(tabs, slot, h, g, r2, e2, lim, e1):
    r2s, e2s, lims, e1s = tabs
    r2s[slot, h, g] = r2.astype(BF16)
    e2s[slot, h, g] = e2.astype(BF16)
    lims[slot, h, g] = lim
    e1s[slot, h, g] = e1


def _gate_act_block(prod, tabs, slot, row0, il, g):
    r2s, e2s, lims, e1s = tabs
    gate = None
    for h in range(PH):
        lim = lims[slot, h, g, pl.ds(row0, NI), :][il:il + 1, :].astype(BF16)
        e1 = e1s[slot, h, g, pl.ds(row0, NI), :][il:il + 1, :].astype(BF16)
        kept = jnp.minimum(e2s[slot, h, g],
                           jnp.maximum(lim - r2s[slot, h, g], jnp.zeros((), BF16)))
        term = e1 * kept
        gate = term if gate is None else gate + term
    half = 0.5 * prod
    act = half + half * lax.erf(prod * SQRT_HALF)
    return (gate.astype(F32) * act).astype(BF16)


def _peer_kernel(qp_ref, k1_ref, k2_ref, h2t_ref, u_ref, vt_ref, o_ref,
                 r2s, e2s, lims, e1s, acc_ref):
    tp = pl.program_id(0)
    e = pl.program_id(1)
    tabs = (r2s, e2s, lims, e1s)
    slot_w = tp % 2
    slot_r = 1 - slot_w
    h = e % PH
    groups = [(e // PH) * UNITS + u for u in range(UNITS)]

    def route_values():
        return [_route_math_distinct(*_route_scores(qp_ref, k1_ref, k2_ref, h, g)) for g in groups]

    def route_finish(results):
        for g, (outs, _) in zip(groups, results):
            _route_store(tabs, slot_w, h, g, *outs)
        for g, (_, clean) in zip(groups, results):
            @pl.when(jnp.min(clean) < 0.5)
            def _(g=g):
                s1, s2 = _route_scores(qp_ref, k1_ref, k2_ref, h, g)
                _route_store(tabs, slot_w, h, g, *_route_math(s1, s2))

    @pl.when(tp == 0)
    def _():
        route_finish(route_values())

    @pl.when(tp > 0)
    def _():
        @pl.when(e == 0)
        def _():
            acc_ref[...] = jnp.zeros_like(acc_ref)

        row0 = pl.multiple_of(e * NI, NI)
        results = route_values()
        pieces = []
        for k in range(TE // PCH):
            prod = _dot(u_ref[k * PCH:(k + 1) * PCH, :], h2t_ref[...])
            for ik in range(PCH // NK):
                pieces.append(jnp.concatenate(
                    [_gate_act_block(prod[ik * NK:(ik + 1) * NK, g * LG:(g + 1) * LG], tabs, slot_r,
                                     row0, k * (PCH // NK) + ik, g)
                     for g in range(NG)], axis=1))
        acc_ref[...] += _dot(vt_ref[...], jnp.concatenate(pieces, axis=0))
        route_finish(results)

        @pl.when(e == NET - 1)
        def _():
            for c in range(D // LG):
                o_ref[:, c * LG:(c + 1) * LG] = acc_ref[c * LG:(c + 1) * LG, :].T


def _peer(qp, k1, k2, h2t, w_u, w_vt):
    tok_r = lambda t: jnp.minimum(t, NTP - 1)
    tok_p = lambda t: jnp.maximum(t - 1, 0)
    return pl.pallas_call(
        _peer_kernel,
        out_shape=jax.ShapeDtypeStruct((T, D), F32),
        grid=(NTP + 1, NET),
        in_specs=[pl.BlockSpec((2 * PH, TMP, NK), lambda t, e: (0, tok_r(t), 0),
                               pipeline_mode=pl.Buffered(1)),
                  _const((PH, NK, NK)), _const((PH, NK, NK)),
                  pl.BlockSpec((D, TMP), lambda t, e: (0, tok_p(t)), pipeline_mode=pl.Buffered(1)),
                  pl.BlockSpec((TE, D), lambda t, e: (jnp.where(t == 0, 0, e), 0)),
                  pl.BlockSpec((D, TE), lambda t, e: (0, jnp.where(t == 0, 0, e)))],
        out_specs=pl.BlockSpec((TMP, D), lambda t, e: (tok_p(t), 0)),
        scratch_shapes=[pltpu.VMEM((2, PH, NG, NK, LG), BF16), pltpu.VMEM((2, PH, NG, NK, LG), BF16),
                        pltpu.VMEM((2, PH, NG, NK, LG), F32), pltpu.VMEM((2, PH, NG, NK, LG), F32),
                        pltpu.VMEM((D, TMP), F32)],
        compiler_params=_params(("arbitrary", "arbitrary"), VMEM_BIG),
        name="peer",
    )(qp, k1, k2, h2t, w_u, w_vt)


def _final_kernel(x1_ref, pe_ref, gt_ref, gf_ref, yp_ref, ys_ref):
    i = pl.program_id(0)
    gt = _group_rows(gt_ref, i, NPT, TM // SS)
    x2 = x1_ref[...] + _per_group(pe_ref[...], lambda pp, g: g * pp, gt)
    y = _rms(x2, gf_ref[...])

    @pl.when(i < NPT)
    def _():
        yp_ref[...] = y

    ys_ref[...] = y


def _final(x1, pe, mod, g_f):
    return pl.pallas_call(
        _final_kernel,
        out_shape=(jax.ShapeDtypeStruct((TP, D), F32), jax.ShapeDtypeStruct((TS, D), F32)),
        grid=(NT_,),
        in_specs=[_tok(D), _tok(D), _mod_spec(5), _const((1, D))],
        out_specs=(_tok_p(D), _tok_s(D)),
        compiler_params=_params(("arbitrary",), VMEM_BIG),
        name="final",
    )(x1, pe, mod, g_f)


W_IN_COLS = 256


def _split_w_in_kernel(wt_ref, lat_ref, hbc_ref, g_ref):
    o_kr = QL + KVL
    o_h = o_kr + DR
    half = DR // 2
    lat_ref[0:o_h, :] = wt_ref[0:o_h, :].astype(BF16)
    lat_ref[o_h:o_h + half, :] = (-wt_ref[o_kr + half:o_h, :]).astype(BF16)
    lat_ref[o_h + half:o_h + DR, :] = wt_ref[o_kr:o_kr + half, :].astype(BF16)
    hbc_ref[...] = wt_ref[o_h:o_h + 3 * CW, :].astype(BF16)
    g_ref[...] = wt_ref[o_h + 3 * CW:, :].astype(BF16)


def _split_w_in(wt):
    cols, rows = wt.shape
    o_h = QL + KVL + DR
    widths = (o_h + DR, 3 * CW, cols - o_h - 3 * CW)
    return pl.pallas_call(
        _split_w_in_kernel,
        out_shape=tuple(jax.ShapeDtypeStruct((n, rows), BF16) for n in widths),
        grid=(rows // W_IN_COLS,),
        in_specs=[pl.BlockSpec((cols, W_IN_COLS), lambda i: (0, i))],
        out_specs=tuple(pl.BlockSpec((n, W_IN_COLS), lambda i: (0, i)) for n in widths),
        compiler_params=_params(("arbitrary",), VMEM_BIG),
        name="split_w_in",
    )(wt)


def _rot_cols(w):
    half = w.shape[-1] // 2
    return jnp.concatenate([-w[..., half:], w[..., :half]], axis=-1)


def _rope_tables():
    half = DR // 2
    pos = jnp.concatenate([jnp.arange(TP), PAST + jnp.tile(jnp.arange(SS), NB)])
    inv = 1.0 / (ROPE_THETA ** (jnp.arange(half, dtype=F32) / half))
    ang = pos.astype(F32)[:, None] * inv[None, :]
    cos = jnp.cos(ang)
    sin = jnp.sin(ang)
    return jnp.concatenate([cos, cos], axis=1), jnp.concatenate([sin, sin], axis=1)


def kernel(x_prompt, x_sample, cache_ckv, cache_krope, state_conv, c_prompt, c_sample, w_ada, b_ada, g_n1, w_in, g_q, g_kv, w_uq, w_uk, w_uv, w_oa, w_conv, b_conv, w_ob, w_o, g_n2, w_pq, sub_k1, sub_k2, w_u, w_v, g_f):
    assert x_prompt.shape == (1, TP, D) and x_sample.shape == (NB, SS, D)
    assert cache_ckv.shape == (1, NB, PAST, KVL) and w_u.shape == (1, NE, D)
    xp = x_prompt.reshape(TP, D)
    xs = x_sample.reshape(TS, D)
    c_all = jnp.concatenate([c_sample, c_prompt, jnp.zeros((MODROWS - NB - 1, D), F32)], axis=0)
    cos2, sin2 = _rope_tables()

    w_lat, w_hbc, w_g = _split_w_in(w_in[0].T)
    wq = w_uq[0]
    w_q3 = jnp.concatenate([wq, _rot_cols(wq[..., DN:])], axis=-1).astype(BF16)
    w_q = w_q3.reshape(QL, NH * QW)
    w_qt = jnp.transpose(w_q3, (1, 2, 0))
    w_uk2 = w_uk[0].reshape(KVL, NH * DN).astype(BF16)
    w_ukt = jnp.transpose(w_uk[0], (1, 2, 0)).astype(BF16)
    w_uv2 = w_uv[0].reshape(KVL, NH * DV).astype(BF16)
    w_uvt = jnp.transpose(w_uv[0], (1, 2, 0)).astype(BF16)
    state = state_conv[0]
    s1 = jnp.pad(state[:, 1:2], ((0, 0), (0, SS - 1), (0, 0))).reshape(TS, CW)
    s2 = jnp.pad(state, ((0, 0), (0, SS - 2), (0, 0))).reshape(TS, CW)

    mod = _ada(c_all, w_ada[0], b_ada)
    h, cq, ckv, ckvb, kr = _lat(xp, xs, mod, g_n1, w_lat, g_q, g_kv, cos2, sin2)
    u, zs, zp = _conv(h, w_hbc, w_conv[0], b_conv, s1, s2)
    g = _gate(h, w_g)
    q, qt, k, vt = _qkv(cq, ckvb, kr, cos2, sin2, cos2.T, sin2.T, w_q, w_qt, w_uk2, w_uvt)
    o_p = _attn(qt, k, vt)
    q_abs = _sq(q, w_ukt)
    o_lat = _sattn(q_abs, q, cache_ckv, jnp.swapaxes(cache_krope, 2, 3), ckvb, kr)
    o_s = _so(o_lat, w_uv2)
    x1, h2, qp = _post(o_p, o_s, u, g, xp, xs, mod, g_n2, w_oa[0].astype(BF16),
                       w_ob[0].astype(BF16), w_o[0].astype(BF16), w_pq[0].astype(BF16))
    pe = _peer(qp, sub_k1[0].astype(BF16), sub_k2[0].astype(BF16), h2,
               w_u[0].astype(BF16), jnp.transpose(w_v[0]).astype(BF16))
    y_p, y_s = _final(x1, pe, mod, g_f.reshape(1, D))

    return (y_p.reshape(1, TP, D), y_s.reshape(NB, SS, D),
            ckv[:TP].reshape(1, 1, TP, KVL), kr[:TP].reshape(1, 1, TP, DR),
            zp[6:8].reshape(1, 1, 2, CW),
            ckv[TP:].reshape(1, NB, SS, KVL), kr[TP:].reshape(1, NB, SS, DR),
            zs.reshape(NB, SS, CW)[:, SS - 2:].reshape(1, NB, 2, CW))
```
